```python
import jax, jax.numpy as jnp
from jax import lax
import numpy as np

D_MODEL = 2048
BATCH = 4
SEQ = 4096
DEPTH = 2

HEAD_DIM = 128
ROPE_THETA = 10000.0
GRID_W = 64
Q_BLOCK = 128
EPS = 1e-6
NEG = -1e30

A_HEADS = 4
A_Q_RANK = 512
A_KV_RANK = 512
A_NOPE = 128
A_ROPE = 64
A_V = 128
B_HEADS = 6
B_PATTERNS = ((128, 1), (512, 4), (2048, 16))
B_BLOCK = 64
C_HEADS = 6
C_KV_HEADS = 2
C_GROUP = C_HEADS // C_KV_HEADS

A_WIDTH = A_HEADS * A_V
B_WIDTH = B_HEADS * HEAD_DIM
C_WIDTH = C_HEADS * HEAD_DIM
MIX_WIDTH = A_WIDTH + B_WIDTH + C_WIDTH

IN_A = A_Q_RANK + A_KV_RANK + A_ROPE
IN_B = 3 * B_WIDTH
IN_C = C_WIDTH + 2 * C_KV_HEADS * HEAD_DIM
IN_WIDTH = IN_A + IN_B + IN_C

D_FF = -(-8 * D_MODEL // (3 * 256)) * 256

kernel_name = "hymba_mla_dilated_axial_gqa_encoder"


def rms_norm(x, g):
    xf = x.astype(jnp.float32)
    y = xf * lax.rsqrt(jnp.mean(xf * xf, axis=-1, keepdims=True) + EPS)
    return (y * g.astype(jnp.float32)).astype(x.dtype)


def rope_angles(pos, dim):
    inv = ROPE_THETA ** (-jnp.arange(0, dim, 2, dtype=jnp.float32) / dim)
    return pos.astype(jnp.float32)[:, None] * inv[None, :]


def apply_rope(x, ang):
    cos = jnp.cos(ang)[:, None, :]
    sin = jnp.sin(ang)[:, None, :]
    x1, x2 = jnp.split(x.astype(jnp.float32), 2, axis=-1)
    out = jnp.concatenate([x1 * cos - x2 * sin, x2 * cos + x1 * sin], axis=-1)
    return out.astype(x.dtype)


def dense_block_attention(q, k, v, scale):
    B, S, Hkv, G, Dk = q.shape
    nb = S // Q_BLOCK
    qb = jnp.moveaxis(q.reshape(B, nb, Q_BLOCK, Hkv, G, Dk), 1, 0)

    def attend(qblk):
        s = jnp.einsum('bqhgd,bkhd->bhgqk', qblk, k, preferred_element_type=jnp.float32) * scale
        p = jax.nn.softmax(s, axis=-1).astype(v.dtype)
        return jnp.einsum('bhgqk,bkhd->bqhgd', p, v)

    o = lax.map(attend, qb)
    return jnp.moveaxis(o, 0, 1).reshape(B, S, Hkv, G, v.shape[-1])


def dilated_pattern(q, k, v, window, dilation):
    B, S, H, D = q.shape
    half = window // (2 * dilation)
    L = S // dilation
    nb = -(-L // B_BLOCK)
    Lp = nb * B_BLOCK

    def to_res(t):
        return jnp.moveaxis(t.reshape(B, L, dilation, H, D), 2, 1)

    qr = jnp.pad(to_res(q), ((0, 0), (0, 0), (0, Lp - L), (0, 0), (0, 0)))
    kpad = ((0, 0), (0, 0), (B_BLOCK, Lp - L + B_BLOCK), (0, 0), (0, 0))
    kr = jnp.pad(to_res(k), kpad).reshape(B, dilation, nb + 2, B_BLOCK, H, D)
    vr = jnp.pad(to_res(v), kpad).reshape(B, dilation, nb + 2, B_BLOCK, H, D)
    qb = qr.reshape(B, dilation, nb, B_BLOCK, H, D)
    kb = jnp.concatenate([kr[:, :, :-2], kr[:, :, 1:-1], kr[:, :, 2:]], axis=3)
    vb = jnp.concatenate([vr[:, :, :-2], vr[:, :, 1:-1], vr[:, :, 2:]], axis=3)

    qi = jnp.arange(nb)[:, None] * B_BLOCK + jnp.arange(B_BLOCK)[None, :]
    kj = jnp.arange(nb)[:, None] * B_BLOCK - B_BLOCK + jnp.arange(3 * B_BLOCK)[None, :]
    rel = kj[:, None, :] - qi[:, :, None]
    mask = (jnp.abs(rel) <= half) & (kj[:, None, :] >= 0) & (kj[:, None, :] < L)

    s = jnp.einsum('brnqhd,brnkhd->brnhqk', qb, kb, preferred_element_type=jnp.float32) * (D ** -0.5)
    s = jnp.where(mask[:, None, :, :], s, NEG)
    m = jnp.max(s, axis=-1, keepdims=True)
    e = jnp.exp(s - m)
    den = jnp.sum(e, axis=-1)
    o = jnp.einsum('brnhqk,brnkhd->brnqhd', e.astype(v.dtype), vb, preferred_element_type=jnp.float32)
    o = o / jnp.moveaxis(den, -1, -2)[..., None]
    lse = jnp.moveaxis(m[..., 0] + jnp.log(den), -1, -2)

    def from_res(t):
        t = t.reshape((B, dilation, Lp) + t.shape[4:])[:, :, :L]
        return jnp.moveaxis(t, 1, 2).reshape((B, S) + t.shape[3:])

    return from_res(o), from_res(lse)


def mixer_mla(pa, q_norm, w_uq, kv_norm, w_ukv, ang_a):
    B, S, _ = pa.shape
    c_q, c_kv, k_rope = jnp.split(pa, [A_Q_RANK, A_Q_RANK + A_KV_RANK], axis=-1)
    q = (rms_norm(c_q, q_norm) @ w_uq).reshape(B, S, A_HEADS, A_NOPE + A_ROPE)
    q_nope, q_rope = jnp.split(q, [A_NOPE], axis=-1)
    q_rope = apply_rope(q_rope, ang_a)
    k_rope = apply_rope(k_rope[:, :, None, :], ang_a)
    kv = (rms_norm(c_kv, kv_norm) @ w_ukv).reshape(B, S, A_HEADS, A_NOPE + A_V)
    k_nope, v = jnp.split(kv, [A_NOPE], axis=-1)
    q_full = jnp.concatenate([q_nope, q_rope], axis=-1)[:, :, :, None, :]
    k_full = jnp.concatenate([k_nope, jnp.broadcast_to(k_rope, (B, S, A_HEADS, A_ROPE))], axis=-1)
    o = dense_block_attention(q_full, k_full, v, (A_NOPE + A_ROPE) ** -0.5)
    return o.reshape(B, S, A_WIDTH)


def mixer_dilated(pb, ang_1d):
    B, S, _ = pb.shape
    q, k, v = [t.reshape(B, S, B_HEADS, HEAD_DIM) for t in jnp.split(pb, 3, axis=-1)]
    q = apply_rope(q, ang_1d)
    k = apply_rope(k, ang_1d)
    outs, lses = [], []
    for window, dilation in B_PATTERNS:
        o, lse = dilated_pattern(q, k, v, window, dilation)
        outs.append(o)
        lses.append(lse)
    w = jax.nn.softmax(jnp.stack(lses, axis=0), axis=0)
    o = jnp.sum(w[..., None] * jnp.stack(outs, axis=0), axis=0)
    return o.astype(pb.dtype).reshape(B, S, B_WIDTH)


def mixer_axial_gqa(pc, q_norm, k_norm, ang_row, ang_col):
    B, S, _ = pc.shape
    q, k, v = jnp.split(pc, [C_WIDTH, C_WIDTH + C_KV_HEADS * HEAD_DIM], axis=-1)
    q = rms_norm(q.reshape(B, S, C_HEADS, HEAD_DIM), q_norm)
    k = rms_norm(k.reshape(B, S, C_KV_HEADS, HEAD_DIM), k_norm)
    v = v.reshape(B, S, C_KV_HEADS, HEAD_DIM)
    hd = HEAD_DIM // 2

    def axial(t):
        return jnp.concatenate([apply_rope(t[..., :hd], ang_row), apply_rope(t[..., hd:], ang_col)], axis=-1)

    q = axial(q).reshape(B, S, C_KV_HEADS, C_GROUP, HEAD_DIM)
    k = axial(k)
    o = dense_block_attention(q, k, v, HEAD_DIM ** -0.5)
    return o.reshape(B, S, C_WIDTH)


def setup_inputs(seed: int = 0) -> dict:
    key = jax.random.key(seed)
    ks = jax.random.split(key, 20)
    f32 = jnp.float32

    def nrm(k, shape, scale):
        return jax.random.normal(k, shape, f32) * scale

    def gain(k, shape):
        return 1.0 + 0.02 * jax.random.normal(k, shape, f32)

    return {
        "x": jax.random.normal(ks[0], (BATCH, SEQ, D_MODEL), f32),
        "attn_norm": gain(ks[1], (DEPTH, D_MODEL)),
        "w_in": nrm(ks[2], (DEPTH, D_MODEL, IN_WIDTH), D_MODEL ** -0.5),
        "a_q_norm": gain(ks[3], (DEPTH, A_Q_RANK)),
        "a_w_uq": nrm(ks[4], (DEPTH, A_Q_RANK, A_HEADS * (A_NOPE + A_ROPE)), A_Q_RANK ** -0.5),
        "a_kv_norm": gain(ks[5], (DEPTH, A_KV_RANK)),
        "a_w_ukv": nrm(ks[6], (DEPTH, A_KV_RANK, A_HEADS * (A_NOPE + A_V)), A_KV_RANK ** -0.5),
        "c_q_norm": gain(ks[7], (DEPTH, HEAD_DIM)),
        "c_k_norm": gain(ks[8], (DEPTH, HEAD_DIM)),
        "out_norm": gain(ks[9], (DEPTH, MIX_WIDTH)),
        "w_out": nrm(ks[10], (DEPTH, MIX_WIDTH, D_MODEL), MIX_WIDTH ** -0.5),
        "ffn_norm": gain(ks[11], (DEPTH, D_MODEL)),
        "w_gate": nrm(ks[12], (DEPTH, D_MODEL, D_FF), D_MODEL ** -0.5),
        "w_up": nrm(ks[13], (DEPTH, D_MODEL, D_FF), D_MODEL ** -0.5),
        "w_down": nrm(ks[14], (DEPTH, D_FF, D_MODEL), D_FF ** -0.5),
        "final_norm": gain(ks[15], (D_MODEL,)),
    }


def reference(x, attn_norm, w_in, a_q_norm, a_w_uq, a_kv_norm, a_w_ukv, c_q_norm, c_k_norm,
              out_norm, w_out, ffn_norm, w_gate, w_up, w_down, final_norm):
    B, S, _ = x.shape
    rows = S // GRID_W
    pos = jnp.arange(S, dtype=jnp.int32)
    row = jnp.repeat(jnp.arange(rows, dtype=jnp.int32), GRID_W)
    col = jnp.tile(jnp.arange(GRID_W, dtype=jnp.int32), rows)
    ang_1d = rope_angles(pos, HEAD_DIM)
    ang_a = rope_angles(pos, A_ROPE)
    ang_row = rope_angles(row, HEAD_DIM // 2)
    ang_col = rope_angles(col, HEAD_DIM // 2)

    for l in range(DEPTH):
        h = rms_norm(x, attn_norm[l])
        proj = jnp.einsum('bsd,de->bse', h, w_in[l])
        pa, pb, pc = jnp.split(proj, [IN_A, IN_A + IN_B], axis=-1)
        ya = mixer_mla(pa, a_q_norm[l], a_w_uq[l], a_kv_norm[l], a_w_ukv[l], ang_a)
        yb = mixer_dilated(pb, ang_1d)
        yc = mixer_axial_gqa(pc, c_q_norm[l], c_k_norm[l], ang_row, ang_col)
        g = out_norm[l]
        y = jnp.concatenate([
            rms_norm(ya, g[:A_WIDTH]),
            rms_norm(yb, g[A_WIDTH:A_WIDTH + B_WIDTH]),
            rms_norm(yc, g[A_WIDTH + B_WIDTH:]),
        ], axis=-1).astype(x.dtype)
        x = x + jnp.einsum('bse,ed->bsd', y, w_out[l])
        h = rms_norm(x, ffn_norm[l])
        ff = jax.nn.silu(h @ w_gate[l]) * (h @ w_up[l])
        x = x + ff @ w_down[l]
    return rms_norm(x, final_norm)
```

```python
import functools

import jax
import jax.numpy as jnp
from jax import lax
from jax.experimental import pallas as pl
from jax.experimental.pallas import tpu as pltpu

F32 = jnp.float32
BF16 = jnp.bfloat16

D_MODEL = 2048
SEQ = 4096
HEAD_DIM = 128
ROPE_THETA = 10000.0
GRID_W = 64
EPS = 1e-6
NEG = -1e30

A_HEADS = 4
A_RANK = 512
A_NOPE = 128
A_ROPE = 64
A_DK = 256
B_HEADS = 6
B_DILATIONS = (1, 4, 16)
B_HALF = 64
C_HEADS = 6
C_KV_HEADS = 2
C_GROUP = C_HEADS // C_KV_HEADS
A_WIDTH = A_HEADS * HEAD_DIM
B_WIDTH = B_HEADS * HEAD_DIM
C_WIDTH = C_HEADS * HEAD_DIM
C_KV_WIDTH = C_KV_HEADS * HEAD_DIM
D_FF = 5632

OFF_CQ = 0
OFF_CKV = 512
OFF_KR = 1024
OFF_BQ = 1152
OFF_BK = OFF_BQ + B_WIDTH
OFF_BV = OFF_BK + B_WIDTH
OFF_CQH = OFF_BV + B_WIDTH
OFF_CK = OFF_CQH + C_WIDTH
OFF_CV = OFF_CK + C_KV_WIDTH
PROJ_USED = OFF_CV + C_KV_WIDTH
PROJ_WIDTH = 5120

SCALE_A = (A_NOPE + A_ROPE) ** -0.5
SCALE_BC = HEAD_DIM ** -0.5

MIB = 1024 * 1024


def _rms(x, g):
    ms = jnp.mean(x * x, axis=-1, keepdims=True)
    return x * lax.rsqrt(ms + EPS) * g


def _rope(x, c, s):
    return x * c + pltpu.roll(x, 64, 1) * s


def _in_proj_kernel(x_ref, g_ref, w_ref, o_ref, h_scr):
    @pl.when(pl.program_id(1) == 0)
    def _():
        h_scr[...] = _rms(x_ref[...], g_ref[...]).astype(BF16)

    o_ref[...] = jnp.dot(h_scr[...], w_ref[...], preferred_element_type=F32)


def _in_proj(x, g, w):
    t = x.shape[0]
    bm, bn = 1024, 1280
    return pl.pallas_call(
        _in_proj_kernel,
        grid=(t // bm, PROJ_WIDTH // bn),
        in_specs=[
            pl.BlockSpec((bm, D_MODEL), lambda i, j: (i, 0)),
            pl.BlockSpec((1, D_MODEL), lambda i, j: (0, 0)),
            pl.BlockSpec((D_MODEL, bn), lambda i, j: (0, j)),
        ],
        out_specs=pl.BlockSpec((bm, bn), lambda i, j: (i, j)),
        out_shape=jax.ShapeDtypeStruct((t, PROJ_WIDTH), F32),
        scratch_shapes=[pltpu.VMEM((bm, D_MODEL), BF16)],
        compiler_params=pltpu.CompilerParams(
            dimension_semantics=("parallel", "arbitrary"), vmem_limit_bytes=48 * MIB),
        name="in_proj",
    )(x, g, w)


PREP_BM = 256


def _prep_kernel(p_ref, cos_a, sin_a, cos_b, sin_b, cos_c, sin_c, gq, gkv, gcq, gck, wuq, wukv,
                 qa, ka, va, qb1, kb1, vb1, qb4, kb4, vb4, qb16, kb16, vb16, qc, kc, vc, scr):
    bm = PREP_BM
    ca, sa = cos_a[...], sin_a[...]
    cb, sb = cos_b[...], sin_b[...]
    cc, sc = cos_c[...], sin_c[...]

    cq = _rms(p_ref[:, OFF_CQ:OFF_CQ + A_RANK], gq[...]).astype(BF16)
    q = jnp.dot(cq, wuq[...], preferred_element_type=F32)
    for h in range(A_HEADS):
        lo = h * A_DK
        qa[:, lo:lo + 128] = (q[:, lo:lo + 128] * SCALE_A).astype(BF16)
        qa[:, lo + 128:lo + 256] = (_rope(q[:, lo + 128:lo + 256], ca, sa) * SCALE_A).astype(BF16)
    ckv = _rms(p_ref[:, OFF_CKV:OFF_CKV + A_RANK], gkv[...]).astype(BF16)
    kv = jnp.dot(ckv, wukv[...], preferred_element_type=F32)
    kr = _rope(p_ref[:, OFF_KR:OFF_KR + 128], ca, sa).astype(BF16)
    for h in range(A_HEADS):
        lo = h * A_DK
        ka[:, lo:lo + 128] = kv[:, h * 128:(h + 1) * 128].astype(BF16)
        ka[:, lo + 128:lo + 256] = kr
    va[...] = kv[:, A_WIDTH:2 * A_WIDTH].astype(BF16)

    for off, scale, use_rope, o1, o4, o16 in (
            (OFF_BQ, SCALE_BC, True, qb1, qb4, qb16),
            (OFF_BK, None, True, kb1, kb4, kb16),
            (OFF_BV, None, False, vb1, vb4, vb16)):
        for h in range(B_HEADS):
            x = p_ref[:, off + h * 128:off + (h + 1) * 128]
            if use_rope:
                x = _rope(x, cb, sb)
            if scale is not None:
                x = x * scale
            scr[h] = x
            o1[:, h * 128:(h + 1) * 128] = x.astype(BF16)
        for d, od in ((4, o4), (16, o16)):
            for h in range(B_HEADS):
                for r in range(d):
                    c0 = (h * d + r) * 128
                    od[:, c0:c0 + 128] = scr[h, pl.ds(r, bm // d, stride=d), :].astype(BF16)

    for h in range(C_HEADS):
        x = _rms(p_ref[:, OFF_CQH + h * 128:OFF_CQH + (h + 1) * 128], gcq[...])
        qc[:, h * 128:(h + 1) * 128] = (_rope(x, cc, sc) * SCALE_BC).astype(BF16)
    for h in range(C_KV_HEADS):
        x = _rms(p_ref[:, OFF_CK + h * 128:OFF_CK + (h + 1) * 128], gck[...])
        kc[:, h * 128:(h + 1) * 128] = _rope(x, cc, sc).astype(BF16)
    vc[...] = p_ref[:, OFF_CV:OFF_CV + C_KV_WIDTH].astype(BF16)


def _prep(proj, tables, gq, gkv, gcq, gck, wuq, wukv):
    t = proj.shape[0]
    bm = PREP_BM
    nblk_seq = SEQ // bm
    row = lambda i: (i, 0)
    fixed = lambda i: (0, 0)
    tab = lambda i: (i % nblk_seq, 0)
    out_widths = [A_HEADS * A_DK, A_HEADS * A_DK, A_WIDTH]
    out_shapes = [jax.ShapeDtypeStruct((t, w), BF16) for w in out_widths]
    out_specs = [pl.BlockSpec((bm, w), row) for w in out_widths]
    for d in B_DILATIONS:
        for _ in range(3):
            out_shapes.append(jax.ShapeDtypeStruct((t // d, d * B_WIDTH), BF16))
            out_specs.append(pl.BlockSpec((bm // d, d * B_WIDTH), row))
    for w in (C_WIDTH, C_KV_WIDTH, C_KV_WIDTH):
        out_shapes.append(jax.ShapeDtypeStruct((t, w), BF16))
        out_specs.append(pl.BlockSpec((bm, w), row))
    in_specs = [pl.BlockSpec((bm, PROJ_WIDTH), row)]
    in_specs += [pl.BlockSpec((bm, 128), tab)] * 6
    in_specs += [pl.BlockSpec((1, A_RANK), fixed), pl.BlockSpec((1, A_RANK), fixed),
                 pl.BlockSpec((1, 128), fixed), pl.BlockSpec((1, 128), fixed),
                 pl.BlockSpec((A_RANK, A_HEADS * A_DK), fixed),
                 pl.BlockSpec((A_RANK, 2 * A_WIDTH), fixed)]
    return pl.pallas_call(
        _prep_kernel,
        grid=(t // bm,),
        in_specs=in_specs,
        out_specs=out_specs,
        out_shape=out_shapes,
        scratch_shapes=[pltpu.VMEM((B_HEADS, bm, 128), F32)],
        compiler_params=pltpu.CompilerParams(
            dimension_semantics=("parallel",), vmem_limit_bytes=40 * MIB),
        name="mixer_prep",
    )(proj, *tables, gq, gkv, gcq, gck, wuq, wukv)


def _attn_kernel(q_ref, k_ref, v_ref, o_ref, *, group, dk, dv):
    k = k_ref[0]
    v = v_ref[0]
    for g in range(group):
        q = q_ref[0, :, g * dk:(g + 1) * dk]
        s = lax.dot_general(q, k, (((1,), (1,)), ((), ())), preferred_element_type=F32)
        m = jnp.max(s, axis=-1, keepdims=True)
        p = jnp.exp(s - m)
        l = jnp.sum(p, axis=-1, keepdims=True)
        o = jnp.dot(p.astype(BF16), v, preferred_element_type=F32)
        o_ref[0, :, g * dv:(g + 1) * dv] = o / l


def _attention(q, k, v, *, kv_heads, group, dk, dv, name):
    b = q.shape[0]
    bq = 256
    return pl.pallas_call(
        functools.partial(_attn_kernel, group=group, dk=dk, dv=dv),
        grid=(b, kv_heads, SEQ // bq),
        in_specs=[
            pl.BlockSpec((1, bq, group * dk), lambda bi, h, qi: (bi, qi, h)),
            pl.BlockSpec((1, SEQ, dk), lambda bi, h, qi: (bi, 0, h)),
            pl.BlockSpec((1, SEQ, dv), lambda bi, h, qi: (bi, 0, h)),
        ],
        out_specs=pl.BlockSpec((1, bq, group * dv), lambda bi, h, qi: (bi, qi, h)),
        out_shape=jax.ShapeDtypeStruct((b, SEQ, kv_heads * group * dv), F32),
        compiler_params=pltpu.CompilerParams(
            dimension_semantics=("parallel", "parallel", "arbitrary"),
            vmem_limit_bytes=48 * MIB),
        name=name,
    )(q, k, v)


DIL_BQ = 128
DIL_WIN = DIL_BQ + 2 * B_HALF


def _dil_block(q_ref, k_ref, v_ref, lane0, q0, length):
    if isinstance(q0, int):
        start = min(max(q0 - B_HALF, 0), length - DIL_WIN)
    else:
        start = pl.multiple_of(jnp.clip(q0 - B_HALF, 0, length - DIL_WIN), B_HALF)
    q = q_ref[0, pl.ds(q0, DIL_BQ), lane0:lane0 + 128]
    k = k_ref[0, pl.ds(start, DIL_WIN), lane0:lane0 + 128]
    v = v_ref[0, pl.ds(start, DIL_WIN), lane0:lane0 + 128]
    s = lax.dot_general(q, k, (((1,), (1,)), ((), ())), preferred_element_type=F32)
    rel = (lax.broadcasted_iota(jnp.int32, (DIL_BQ, DIL_WIN), 1)
           - lax.broadcasted_iota(jnp.int32, (DIL_BQ, DIL_WIN), 0)) + (start - q0)
    s = jnp.where(jnp.abs(rel) <= B_HALF, s, NEG)
    m = jnp.max(s, axis=-1, keepdims=True)
    e = jnp.exp(s - m)
    l = jnp.sum(e, axis=-1, keepdims=True)
    o = jnp.dot(e.astype(BF16), v, preferred_element_type=F32) / l
    lse = m + jnp.log(l)
    return o, jnp.broadcast_to(lse, (DIL_BQ, 128))


def _dil_kernel(q1, k1, v1, q4, k4, v4, q16, k16, v16, o_ref,
                osm4, lsm4, osm16, lsm16, otok, ltok):
    def body1(n, carry):
        q0 = pl.multiple_of(n * DIL_BQ, DIL_BQ)
        o, lse = _dil_block(q1, k1, v1, 0, q0, SEQ)
        otok[0, pl.ds(q0, DIL_BQ), :] = o
        ltok[0, pl.ds(q0, DIL_BQ), :] = lse
        return carry

    lax.fori_loop(0, SEQ // DIL_BQ, body1, 0, unroll=4)

    def body4(n, carry):
        q0 = pl.multiple_of(n * DIL_BQ, DIL_BQ)
        for r in range(4):
            o, lse = _dil_block(q4, k4, v4, r * 128, q0, SEQ // 4)
            osm4[pl.ds(q0, DIL_BQ), r * 128:(r + 1) * 128] = o
            lsm4[pl.ds(q0, DIL_BQ), r * 128:(r + 1) * 128] = lse
        return carry

    lax.fori_loop(0, SEQ // 4 // DIL_BQ, body4, 0)
    for r in range(4):
        otok[1, pl.ds(r, SEQ // 4, stride=4), :] = osm4[:, r * 128:(r + 1) * 128]
        ltok[1, pl.ds(r, SEQ // 4, stride=4), :] = lsm4[:, r * 128:(r + 1) * 128]

    for n in range(SEQ // 16 // DIL_BQ):
        q0 = n * DIL_BQ
        for r in range(16):
            o, lse = _dil_block(q16, k16, v16, r * 128, q0, SEQ // 16)
            osm16[q0:q0 + DIL_BQ, r * 128:(r + 1) * 128] = o
            lsm16[q0:q0 + DIL_BQ, r * 128:(r + 1) * 128] = lse
    for r in range(16):
        otok[2, pl.ds(r, SEQ // 16, stride=16), :] = osm16[:, r * 128:(r + 1) * 128]
        ltok[2, pl.ds(r, SEQ // 16, stride=16), :] = lsm16[:, r * 128:(r + 1) * 128]

    chunk = 256

    def combine(c, carry):
        r0 = pl.multiple_of(c * chunk, chunk)
        l0 = ltok[0, pl.ds(r0, chunk), :]
        l1 = ltok[1, pl.ds(r0, chunk), :]
        l2 = ltok[2, pl.ds(r0, chunk), :]
        mx = jnp.maximum(jnp.maximum(l0, l1), l2)
        e0 = jnp.exp(l0 - mx)
        e1 = jnp.exp(l1 - mx)
        e2 = jnp.exp(l2 - mx)
        num = (e0 * otok[0, pl.ds(r0, chunk), :] + e1 * otok[1, pl.ds(r0, chunk), :]
               + e2 * otok[2, pl.ds(r0, chunk), :])
        o_ref[0, pl.ds(r0, chunk), :] = num / (e0 + e1 + e2)
        return carry

    lax.fori_loop(0, SEQ // chunk, combine, 0)


def _dilated(qkv_by_dilation):
    b = qkv_by_dilation[0].shape[0]
    in_specs = []
    for d in B_DILATIONS:
        in_specs += [pl.BlockSpec((1, SEQ // d, d * 128), lambda bi, h: (bi, 0, h))] * 3
    return pl.pallas_call(
        _dil_kernel,
        grid=(b, B_HEADS),
        in_specs=in_specs,
        out_specs=pl.BlockSpec((1, SEQ, 128), lambda bi, h: (bi, 0, h)),
        out_shape=jax.ShapeDtypeStruct((b, SEQ, B_WIDTH), F32),
        scratch_shapes=[
            pltpu.VMEM((SEQ // 4, 4 * 128), F32), pltpu.VMEM((SEQ // 4, 4 * 128), F32),
            pltpu.VMEM((SEQ // 16, 16 * 128), F32), pltpu.VMEM((SEQ // 16, 16 * 128), F32),
            pltpu.VMEM((3, SEQ, 128), F32), pltpu.VMEM((3, SEQ, 128), F32),
        ],
        compiler_params=pltpu.CompilerParams(
            dimension_semantics=("parallel", "parallel"), vmem_limit_bytes=56 * MIB),
        name="dilated_attn",
    )(*qkv_by_dilation)


def _out_proj_kernel(ya, yb, yc, x_ref, g_ref, w_ref, o_ref, y_scr):
    @pl.when(pl.program_id(1) == 0)
    def _():
        y_scr[:, 0:A_WIDTH] = _rms(ya[...], g_ref[:, 0:A_WIDTH]).astype(BF16)
        y_scr[:, A_WIDTH:A_WIDTH + B_WIDTH] = _rms(
            yb[...], g_ref[:, A_WIDTH:A_WIDTH + B_WIDTH]).astype(BF16)
        y_scr[:, A_WIDTH + B_WIDTH:] = _rms(yc[...], g_ref[:, A_WIDTH + B_WIDTH:]).astype(BF16)

    o_ref[...] = x_ref[...] + jnp.dot(y_scr[...], w_ref[...], preferred_element_type=F32)


def _out_proj(ya, yb, yc, x, g, w):
    t = x.shape[0]
    bm, bn = 1024, 1024
    return pl.pallas_call(
        _out_proj_kernel,
        grid=(t // bm, D_MODEL // bn),
        in_specs=[
            pl.BlockSpec((bm, A_WIDTH), lambda i, j: (i, 0)),
            pl.BlockSpec((bm, B_WIDTH), lambda i, j: (i, 0)),
            pl.BlockSpec((bm, C_WIDTH), lambda i, j: (i, 0)),
            pl.BlockSpec((bm, bn), lambda i, j: (i, j)),
            pl.BlockSpec((1, D_MODEL), lambda i, j: (0, 0)),
            pl.BlockSpec((D_MODEL, bn), lambda i, j: (0, j)),
        ],
        out_specs=pl.BlockSpec((bm, bn), lambda i, j: (i, j)),
        out_shape=jax.ShapeDtypeStruct((t, D_MODEL), F32),
        scratch_shapes=[pltpu.VMEM((bm, D_MODEL), BF16)],
        compiler_params=pltpu.CompilerParams(
            dimension_semantics=("parallel", "arbitrary"), vmem_limit_bytes=52 * MIB),
        name="out_proj",
    )(ya, yb, yc, x, g, w)


def _ffn_kernel(x_ref, g_ref, wg_ref, wu_ref, wd_ref, fg_ref, o_ref, h_scr, *, final_norm):
    j = pl.program_id(1)

    @pl.when(j == 0)
    def _():
        x = x_ref[...]
        h_scr[...] = _rms(x, g_ref[...]).astype(BF16)
        o_ref[...] = x

    h = h_scr[...]
    gate = jnp.dot(h, wg_ref[...], preferred_element_type=F32)
    up = jnp.dot(h, wu_ref[...], preferred_element_type=F32)
    ff = (gate * jax.nn.sigmoid(gate)) * up
    o_ref[...] += jnp.dot(ff.astype(BF16), wd_ref[...], preferred_element_type=F32)

    if final_norm:
        @pl.when(j == pl.num_programs(1) - 1)
        def _():
            o_ref[...] = _rms(o_ref[...], fg_ref[...])


def _ffn(x, g, wg, wu, wd, fg, *, final_norm):
    t = x.shape[0]
    bm, bf = 512, 512
    return pl.pallas_call(
        functools.partial(_ffn_kernel, final_norm=final_norm),
        grid=(t // bm, D_FF // bf),
        in_specs=[
            pl.BlockSpec((bm, D_MODEL), lambda i, j: (i, 0)),
            pl.BlockSpec((1, D_MODEL), lambda i, j: (0, 0)),
            pl.BlockSpec((D_MODEL, bf), lambda i, j: (0, j)),
            pl.BlockSpec((D_MODEL, bf), lambda i, j: (0, j)),
            pl.BlockSpec((bf, D_MODEL), lambda i, j: (j, 0)),
            pl.BlockSpec((1, D_MODEL), lambda i, j: (0, 0)),
        ],
        out_specs=pl.BlockSpec((bm, D_MODEL), lambda i, j: (i, 0)),
        out_shape=jax.ShapeDtypeStruct((t, D_MODEL), F32),
        scratch_shapes=[pltpu.VMEM((bm, D_MODEL), BF16)],
        compiler_params=pltpu.CompilerParams(
            dimension_semantics=("parallel", "arbitrary"), vmem_limit_bytes=48 * MIB),
        name="swiglu_ffn",
    )(x, g, wg, wu, wd, fg)


def _rope_tables():
    pos = jnp.arange(SEQ, dtype=jnp.int32)

    def angles(p, dim):
        inv = ROPE_THETA ** (-jnp.arange(0, dim, 2, dtype=F32) / dim)
        return p.astype(F32)[:, None] * inv[None, :]

    ang_b = angles(pos, HEAD_DIM)
    ang_a = angles(pos, A_ROPE)
    ang_r = angles(pos // GRID_W, HEAD_DIM // 2)
    ang_c = angles(pos % GRID_W, HEAD_DIM // 2)
    z = jnp.zeros((SEQ, 32), F32)
    cat = lambda *xs: jnp.concatenate(xs, axis=-1)
    cos_a = cat(jnp.cos(ang_a), z, jnp.cos(ang_a), z)
    sin_a = cat(-jnp.sin(ang_a), z, jnp.sin(ang_a), z)
    cos_b = cat(jnp.cos(ang_b), jnp.cos(ang_b))
    sin_b = cat(-jnp.sin(ang_b), jnp.sin(ang_b))
    cos_c = cat(jnp.cos(ang_r), jnp.cos(ang_c), jnp.cos(ang_r), jnp.cos(ang_c))
    sin_c = cat(-jnp.sin(ang_r), -jnp.sin(ang_c), jnp.sin(ang_r), jnp.sin(ang_c))
    return cos_a, sin_a, cos_b, sin_b, cos_c, sin_c


def _axial_perm():
    a = jnp.arange(32)
    return jnp.concatenate([a, a + 64, a + 32, a + 96])


def _layout_w_in(w):
    z32 = jnp.zeros((D_MODEL, 32), w.dtype)
    kr = w[:, 1024:1088]
    kr_tile = jnp.concatenate([kr[:, :32], z32, kr[:, 32:], z32], axis=1)
    perm = _axial_perm()
    cq = w[:, 3392:4160].reshape(D_MODEL, C_HEADS, 128)[:, :, perm].reshape(D_MODEL, C_WIDTH)
    ck = w[:, 4160:4416].reshape(D_MODEL, C_KV_HEADS, 128)[:, :, perm].reshape(D_MODEL, C_KV_WIDTH)
    pad = jnp.zeros((D_MODEL, PROJ_WIDTH - PROJ_USED), w.dtype)
    out = jnp.concatenate([w[:, 0:1024], kr_tile, w[:, 1088:3392], cq, ck, w[:, 4416:4672], pad],
                          axis=1)
    return out.astype(BF16)


def _layout_w_uq(w):
    w = w.reshape(A_RANK, A_HEADS, A_NOPE + A_ROPE)
    z = jnp.zeros((A_RANK, A_HEADS, 32), w.dtype)
    out = jnp.concatenate([w[:, :, :128], w[:, :, 128:160], z, w[:, :, 160:192], z], axis=2)
    return out.reshape(A_RANK, A_HEADS * A_DK).astype(BF16)


def _layout_w_ukv(w):
    w = w.reshape(A_RANK, A_HEADS, 2 * HEAD_DIM)
    out = jnp.concatenate([w[:, :, :128].reshape(A_RANK, A_WIDTH),
                           w[:, :, 128:].reshape(A_RANK, A_WIDTH)], axis=1)
    return out.astype(BF16)


def kernel(x, attn_norm, w_in, a_q_norm, a_w_uq, a_kv_norm, a_w_ukv, c_q_norm, c_k_norm,
           out_norm, w_out, ffn_norm, w_gate, w_up, w_down, final_norm):
    bsz, seq, _ = x.shape
    t = bsz * seq
    depth = w_in.shape[0]
    tables = _rope_tables()
    perm = _axial_perm()
    xf = x.reshape(t, D_MODEL)
    for l in range(depth):
        proj = _in_proj(xf, attn_norm[l][None, :], _layout_w_in(w_in[l]))
        (qa, ka, va, qb1, kb1, vb1, qb4, kb4, vb4, qb16, kb16, vb16, qc, kc, vc) = _prep(
            proj, tables, a_q_norm[l][None, :], a_kv_norm[l][None, :],
            c_q_norm[l][perm][None, :], c_k_norm[l][perm][None, :],
            _layout_w_uq(a_w_uq[l]), _layout_w_ukv(a_w_ukv[l]))
        r3 = lambda a, n=1: a.reshape(bsz, seq // n, a.shape[1])
        ya = _attention(r3(qa), r3(ka), r3(va), kv_heads=A_HEADS, group=1, dk=A_DK, dv=HEAD_DIM,
                        name="mla_attn")
        yb = _dilated([r3(qb1), r3(kb1), r3(vb1), r3(qb4, 4), r3(kb4, 4), r3(vb4, 4),
                       r3(qb16, 16), r3(kb16, 16), r3(vb16, 16)])
        yc = _attention(r3(qc), r3(kc), r3(vc), kv_heads=C_KV_HEADS, group=C_GROUP, dk=HEAD_DIM,
                        dv=HEAD_DIM, name="gqa_attn")
        xf = _out_proj(ya.reshape(t, A_WIDTH), yb.reshape(t, B_WIDTH), yc.reshape(t, C_WIDTH),
                       xf, out_norm[l][None, :], w_out[l].astype(BF16))
        xf = _ffn(xf, ffn_norm[l][None, :], w_gate[l].astype(BF16), w_up[l].astype(BF16),
                  w_down[l].astype(BF16), final_norm[None, :], final_norm=(l == depth - 1))
    return xf.reshape(bsz, seq, D_MODEL)
```

```python
import functools

import jax
import jax.numpy as jnp
from jax import lax
from jax.experimental import pallas as pl
from jax.experimental.pallas import tpu as pltpu

F32 = jnp.float32
BF16 = jnp.bfloat16

D_MODEL = 2048
SEQ = 4096
HEAD_DIM = 128
ROPE_THETA = 10000.0
GRID_W = 64
EPS = 1e-6
NEG = -1e30

A_HEADS = 4
A_RANK = 512
A_NOPE = 128
A_ROPE = 64
A_DK = 256
B_HEADS = 6
B_DILATIONS = (1, 4, 16)
B_HALF = 64
C_HEADS = 6
C_KV_HEADS = 2
C_GROUP = C_HEADS // C_KV_HEADS
A_WIDTH = A_HEADS * HEAD_DIM
B_WIDTH = B_HEADS * HEAD_DIM
C_WIDTH = C_HEADS * HEAD_DIM
C_KV_WIDTH = C_KV_HEADS * HEAD_DIM
D_FF = 5632

OFF_CQ = 0
OFF_CKV = 512
OFF_KR = 1024
OFF_BQ = 1152
OFF_BK = OFF_BQ + B_WIDTH
OFF_BV = OFF_BK + B_WIDTH
OFF_CQH = OFF_BV + B_WIDTH
OFF_CK = OFF_CQH + C_WIDTH
OFF_CV = OFF_CK + C_KV_WIDTH
PROJ_USED = OFF_CV + C_KV_WIDTH
PROJ_WIDTH = PROJ_USED

LOG2E = 1.4426950408889634
SCALE_A = (A_NOPE + A_ROPE) ** -0.5 * LOG2E
SCALE_B = HEAD_DIM ** -0.5
SCALE_C = HEAD_DIM ** -0.5 * LOG2E

MIB = 1024 * 1024


def _rms(x, g):
    ms = jnp.mean(x * x, axis=-1, keepdims=True)
    return x * lax.rsqrt(ms + EPS) * g


def _rope(x, c, s):
    return x * c + pltpu.roll(x, 64, 1) * s


def _resident(shape):
    return pl.BlockSpec(shape, lambda i: (0,) * len(shape), pipeline_mode=pl.Buffered(1))


def _in_proj_kernel(x_ref, g_ref, w_ref, o_ref):
    h = _rms(x_ref[...], g_ref[...]).astype(BF16)
    o_ref[...] = jnp.dot(h, w_ref[...], preferred_element_type=F32)


def _in_proj(x, g, w):
    t = x.shape[0]
    bm = 512
    return pl.pallas_call(
        _in_proj_kernel,
        grid=(t // bm,),
        in_specs=[
            pl.BlockSpec((bm, D_MODEL), lambda i: (i, 0)),
            _resident((1, D_MODEL)),
            _resident((D_MODEL, PROJ_WIDTH)),
        ],
        out_specs=pl.BlockSpec((bm, PROJ_WIDTH), lambda i: (i, 0)),
        out_shape=jax.ShapeDtypeStruct((t, PROJ_WIDTH), F32),
        compiler_params=pltpu.CompilerParams(
            dimension_semantics=("parallel",), vmem_limit_bytes=56 * MIB),
        name="in_proj",
    )(x, g, w)


PREP_BM = 256


def _prep_kernel(p_ref, cos_a, sin_a, cos_b, sin_b, cos_c, sin_c, gq, gkv, gcq, gck, wuq, wukv,
                 qa, ka, va, qb1, kb1, vb1, qb4, kb4, vb4, qb16, kb16, vb16, qc, kc, vc, scr):
    bm = PREP_BM
    ca, sa = cos_a[...], sin_a[...]
    cb, sb = cos_b[...], sin_b[...]
    cc, sc = cos_c[...], sin_c[...]

    cq = _rms(p_ref[:, OFF_CQ:OFF_CQ + A_RANK], gq[...]).astype(BF16)
    q = jnp.dot(cq, wuq[...], preferred_element_type=F32)
    for h in range(A_HEADS):
        lo = h * A_DK
        qa[:, lo:lo + 128] = (q[:, lo:lo + 128] * SCALE_A).astype(BF16)
        qa[:, lo + 128:lo + 256] = (_rope(q[:, lo + 128:lo + 256], ca, sa) * SCALE_A).astype(BF16)
    ckv = _rms(p_ref[:, OFF_CKV:OFF_CKV + A_RANK], gkv[...]).astype(BF16)
    kv = jnp.dot(ckv, wukv[...], preferred_element_type=F32)
    kr = _rope(p_ref[:, OFF_KR:OFF_KR + 128], ca, sa).astype(BF16)
    for h in range(A_HEADS):
        lo = h * A_DK
        ka[:, lo:lo + 128] = kv[:, h * 128:(h + 1) * 128].astype(BF16)
        ka[:, lo + 128:lo + 256] = kr
    va[...] = kv[:, A_WIDTH:2 * A_WIDTH].astype(BF16)

    for off, scale, use_rope, o1, o4, o16 in (
            (OFF_BQ, SCALE_B, True, qb1, qb4, qb16),
            (OFF_BK, None, True, kb1, kb4, kb16),
            (OFF_BV, None, False, vb1, vb4, vb16)):
        for h in range(B_HEADS):
            x = p_ref[:, off + h * 128:off + (h + 1) * 128]
            if use_rope:
                x = _rope(x, cb, sb)
            if scale is not None:
                x = x * scale
            scr[h] = x
            o1[:, h * 128:(h + 1) * 128] = x.astype(BF16)
        for d, od in ((4, o4), (16, o16)):
            for h in range(B_HEADS):
                for r in range(d):
                    c0 = (h * d + r) * 128
                    od[:, c0:c0 + 128] = scr[h, pl.ds(r, bm // d, stride=d), :].astype(BF16)

    for h in range(C_HEADS):
        x = _rms(p_ref[:, OFF_CQH + h * 128:OFF_CQH + (h + 1) * 128], gcq[...])
        qc[:, h * 128:(h + 1) * 128] = (_rope(x, cc, sc) * SCALE_C).astype(BF16)
    for h in range(C_KV_HEADS):
        x = _rms(p_ref[:, OFF_CK + h * 128:OFF_CK + (h + 1) * 128], gck[...])
        kc[:, h * 128:(h + 1) * 128] = _rope(x, cc, sc).astype(BF16)
    vc[...] = p_ref[:, OFF_CV:OFF_CV + C_KV_WIDTH].astype(BF16)


def _prep(proj, tables, gq, gkv, gcq, gck, wuq, wukv):
    t = proj.shape[0]
    bm = PREP_BM
    nblk_seq = SEQ // bm
    row = lambda i: (i, 0)
    fixed = lambda i: (0, 0)
    tab = lambda i: (i % nblk_seq, 0)
    out_widths = [A_HEADS * A_DK, A_HEADS * A_DK, A_WIDTH]
    out_shapes = [jax.ShapeDtypeStruct((t, w), BF16) for w in out_widths]
    out_specs = [pl.BlockSpec((bm, w), row) for w in out_widths]
    for d in B_DILATIONS:
        for _ in range(3):
            out_shapes.append(jax.ShapeDtypeStruct((t // d, d * B_WIDTH), BF16))
            out_specs.append(pl.BlockSpec((bm // d, d * B_WIDTH), row))
    for w in (C_WIDTH, C_KV_WIDTH, C_KV_WIDTH):
        out_shapes.append(jax.ShapeDtypeStruct((t, w), BF16))
        out_specs.append(pl.BlockSpec((bm, w), row))
    in_specs = [pl.BlockSpec((bm, PROJ_WIDTH), row)]
    in_specs += [pl.BlockSpec((bm, 128), tab)] * 6
    in_specs += [pl.BlockSpec((1, A_RANK), fixed), pl.BlockSpec((1, A_RANK), fixed),
                 pl.BlockSpec((1, 128), fixed), pl.BlockSpec((1, 128), fixed),
                 pl.BlockSpec((A_RANK, A_HEADS * A_DK), fixed),
                 pl.BlockSpec((A_RANK, 2 * A_WIDTH), fixed)]
    return pl.pallas_call(
        _prep_kernel,
        grid=(t // bm,),
        in_specs=in_specs,
        out_specs=out_specs,
        out_shape=out_shapes,
        scratch_shapes=[pltpu.VMEM((B_HEADS, bm, 128), F32)],
        compiler_params=pltpu.CompilerParams(
            dimension_semantics=("parallel",), vmem_limit_bytes=40 * MIB),
        name="mixer_prep",
    )(proj, *tables, gq, gkv, gcq, gck, wuq, wukv)


ATTN_KC = 1024
ATTN_SUB = 256


def _attn_kernel(q_ref, k_ref, v_ref, o_ref, vt_scr, *, group, dk, dv, bq):
    @pl.when(pl.program_id(2) == 0)
    def _():
        vt_scr[...] = v_ref[0].T

    q_all = jnp.concatenate([q_ref[0, :, g * dk:(g + 1) * dk] for g in range(group)], axis=0)
    n_chunks = SEQ // ATTN_KC
    n_sub = ATTN_KC // ATTN_SUB

    def scores(c, j):
        r0 = c * ATTN_KC + j * ATTN_SUB
        return lax.dot_general(k_ref[0, r0:r0 + ATTN_SUB, :], q_all, (((1,), (1,)), ((), ())),
                               preferred_element_type=F32)

    def col_max(pieces):
        mx = pieces[0]
        for piece in pieces[1:]:
            mx = jnp.maximum(mx, piece)
        return jnp.max(mx, axis=0, keepdims=True)

    s_cur = [scores(0, j) for j in range(n_sub)]
    m = l = acc = None
    for c in range(n_chunks):
        m_c = col_max(s_cur)
        m_new = m_c if c == 0 else jnp.maximum(m, m_c)
        s_next, p_sum, p_bf = [], None, []
        for j in range(n_sub):
            if c + 1 < n_chunks:
                s_next.append(scores(c + 1, j))
            p = jnp.exp2(s_cur[j] - m_new)
            p_sum = p if j == 0 else p_sum + p
            p_bf.append(p.astype(BF16))
        l_c = jnp.sum(p_sum, axis=0, keepdims=True)
        pv = jnp.dot(vt_scr[:, c * ATTN_KC:(c + 1) * ATTN_KC], jnp.concatenate(p_bf, axis=0),
                     preferred_element_type=F32)
        if c == 0:
            l, acc = l_c, pv
        else:
            alpha = jnp.exp2(m - m_new)
            l = alpha * l + l_c
            acc = alpha * acc + pv
        m = m_new
        s_cur = s_next
    out = acc / l
    for g in range(group):
        o_ref[0, :, g * dv:(g + 1) * dv] = out[:, g * bq:(g + 1) * bq].T


def _attention(q, k, v, *, kv_heads, group, dk, dv, bq, name):
    b = q.shape[0]
    return pl.pallas_call(
        functools.partial(_attn_kernel, group=group, dk=dk, dv=dv, bq=bq),
        grid=(b, kv_heads, SEQ // bq),
        in_specs=[
            pl.BlockSpec((1, bq, group * dk), lambda bi, h, qi: (bi, qi, h)),
            pl.BlockSpec((1, SEQ, dk), lambda bi, h, qi: (bi, 0, h)),
            pl.BlockSpec((1, SEQ, dv), lambda bi, h, qi: (bi, 0, h)),
        ],
        out_specs=pl.BlockSpec((1, bq, group * dv), lambda bi, h, qi: (bi, qi, h)),
        out_shape=jax.ShapeDtypeStruct((b, SEQ, kv_heads * group * dv), F32),
        scratch_shapes=[pltpu.VMEM((dv, SEQ), BF16)],
        compiler_params=pltpu.CompilerParams(
            dimension_semantics=("parallel", "parallel", "arbitrary"),
            vmem_limit_bytes=48 * MIB),
        name=name,
    )(q, k, v)


DIL_BQ = 128
DIL_WIN = DIL_BQ + 2 * B_HALF


def _dil_block(q_ref, k_ref, v_ref, lane0, q0, length):
    if isinstance(q0, int):
        start = min(max(q0 - B_HALF, 0), length - DIL_WIN)
    else:
        start = pl.multiple_of(jnp.clip(q0 - B_HALF, 0, length - DIL_WIN), B_HALF)
    q = q_ref[0, pl.ds(q0, DIL_BQ), lane0:lane0 + 128]
    k = k_ref[0, pl.ds(start, DIL_WIN), lane0:lane0 + 128]
    v = v_ref[0, pl.ds(start, DIL_WIN), lane0:lane0 + 128]
    s = lax.dot_general(q, k, (((1,), (1,)), ((), ())), preferred_element_type=F32)
    rel = (lax.broadcasted_iota(jnp.int32, (DIL_BQ, DIL_WIN), 1)
           - lax.broadcasted_iota(jnp.int32, (DIL_BQ, DIL_WIN), 0)) + (start - q0)
    s = jnp.where(jnp.abs(rel) <= B_HALF, s, NEG)
    m = jnp.max(s, axis=-1, keepdims=True)
    e = jnp.exp(s - m)
    l = jnp.sum(e, axis=-1, keepdims=True)
    o = jnp.dot(e.astype(BF16), v, preferred_element_type=F32) / l
    lse = m + jnp.log(l)
    return o, jnp.broadcast_to(lse, (DIL_BQ, 128))


def _dil_kernel(q1, k1, v1, q4, k4, v4, q16, k16, v16, o_ref,
                osm4, lsm4, osm16, lsm16, otok, ltok):
    def body1(n, carry):
        q0 = pl.multiple_of(n * DIL_BQ, DIL_BQ)
        o, lse = _dil_block(q1, k1, v1, 0, q0, SEQ)
        otok[0, pl.ds(q0, DIL_BQ), :] = o
        ltok[0, pl.ds(q0, DIL_BQ), :] = lse
        return carry

    lax.fori_loop(0, SEQ // DIL_BQ, body1, 0, unroll=4)

    def body4(n, carry):
        q0 = pl.multiple_of(n * DIL_BQ, DIL_BQ)
        for r in range(4):
            o, lse = _dil_block(q4, k4, v4, r * 128, q0, SEQ // 4)
            osm4[pl.ds(q0, DIL_BQ), r * 128:(r + 1) * 128] = o
            lsm4[pl.ds(q0, DIL_BQ), r * 128:(r + 1) * 128] = lse
        return carry

    lax.fori_loop(0, SEQ // 4 // DIL_BQ, body4, 0)
    for r in range(4):
        otok[1, pl.ds(r, SEQ // 4, stride=4), :] = osm4[:, r * 128:(r + 1) * 128]
        ltok[1, pl.ds(r, SEQ // 4, stride=4), :] = lsm4[:, r * 128:(r + 1) * 128]

    for n in range(SEQ // 16 // DIL_BQ):
        q0 = n * DIL_BQ
        for r in range(16):
            o, lse = _dil_block(q16, k16, v16, r * 128, q0, SEQ // 16)
            osm16[q0:q0 + DIL_BQ, r * 128:(r + 1) * 128] = o
            lsm16[q0:q0 + DIL_BQ, r * 128:(r + 1) * 128] = lse
    for r in range(16):
        otok[2, pl.ds(r, SEQ // 16, stride=16), :] = osm16[:, r * 128:(r + 1) * 128]
        ltok[2, pl.ds(r, SEQ // 16, stride=16), :] = lsm16[:, r * 128:(r + 1) * 128]

    chunk = 256

    def combine(c, carry):
        r0 = pl.multiple_of(c * chunk, chunk)
        l0 = ltok[0, pl.ds(r0, chunk), :]
        l1 = ltok[1, pl.ds(r0, chunk), :]
        l2 = ltok[2, pl.ds(r0, chunk), :]
        mx = jnp.maximum(jnp.maximum(l0, l1), l2)
        e0 = jnp.exp(l0 - mx)
        e1 = jnp.exp(l1 - mx)
        e2 = jnp.exp(l2 - mx)
        num = (e0 * otok[0, pl.ds(r0, chunk), :] + e1 * otok[1, pl.ds(r0, chunk), :]
               + e2 * otok[2, pl.ds(r0, chunk), :])
        o_ref[0, pl.ds(r0, chunk), :] = num / (e0 + e1 + e2)
        return carry

    lax.fori_loop(0, SEQ // chunk, combine, 0)


def _dilated(qkv_by_dilation):
    b = qkv_by_dilation[0].shape[0]
    in_specs = []
    for d in B_DILATIONS:
        in_specs += [pl.BlockSpec((1, SEQ // d, d * 128), lambda bi, h: (bi, 0, h))] * 3
    return pl.pallas_call(
        _dil_kernel,
        grid=(b, B_HEADS),
        in_specs=in_specs,
        out_specs=pl.BlockSpec((1, SEQ, 128), lambda bi, h: (bi, 0, h)),
        out_shape=jax.ShapeDtypeStruct((b, SEQ, B_WIDTH), F32),
        scratch_shapes=[
            pltpu.VMEM((SEQ // 4, 4 * 128), F32), pltpu.VMEM((SEQ // 4, 4 * 128), F32),
            pltpu.VMEM((SEQ // 16, 16 * 128), F32), pltpu.VMEM((SEQ // 16, 16 * 128), F32),
            pltpu.VMEM((3, SEQ, 128), F32), pltpu.VMEM((3, SEQ, 128), F32),
        ],
        compiler_params=pltpu.CompilerParams(
            dimension_semantics=("parallel", "parallel"), vmem_limit_bytes=56 * MIB),
        name="dilated_attn",
    )(*qkv_by_dilation)


def _out_proj_kernel(ya, yb, yc, x_ref, g_ref, w_ref, o_ref):
    b0, c0 = A_WIDTH, A_WIDTH + B_WIDTH
    na = _rms(ya[...], g_ref[:, 0:b0]).astype(BF16)
    nb = _rms(yb[...], g_ref[:, b0:c0]).astype(BF16)
    nc = _rms(yc[...], g_ref[:, c0:]).astype(BF16)
    y = (jnp.dot(na, w_ref[0:b0, :], preferred_element_type=F32)
         + jnp.dot(nb, w_ref[b0:c0, :], preferred_element_type=F32)
         + jnp.dot(nc, w_ref[c0:, :], preferred_element_type=F32))
    o_ref[...] = x_ref[...] + y


def _out_proj(ya, yb, yc, x, g, w):
    t = x.shape[0]
    bm = 512
    row = lambda i: (i, 0)
    return pl.pallas_call(
        _out_proj_kernel,
        grid=(t // bm,),
        in_specs=[
            pl.BlockSpec((bm, A_WIDTH), row),
            pl.BlockSpec((bm, B_WIDTH), row),
            pl.BlockSpec((bm, C_WIDTH), row),
            pl.BlockSpec((bm, D_MODEL), row),
            _resident((1, D_MODEL)),
            _resident((D_MODEL, D_MODEL)),
        ],
        out_specs=pl.BlockSpec((bm, D_MODEL), row),
        out_shape=jax.ShapeDtypeStruct((t, D_MODEL), F32),
        compiler_params=pltpu.CompilerParams(
            dimension_semantics=("parallel",), vmem_limit_bytes=48 * MIB),
        name="out_proj",
    )(ya, yb, yc, x, g, w)


def _ffn_kernel(x_ref, g_ref, wg_ref, wu_ref, wd_ref, fg_ref, o_ref, h_scr, *, final_norm):
    j = pl.program_id(1)

    @pl.when(j == 0)
    def _():
        x = x_ref[...]
        h_scr[...] = _rms(x, g_ref[...]).astype(BF16)
        o_ref[...] = x

    h = h_scr[...]
    gate = jnp.dot(h, wg_ref[...], preferred_element_type=F32)
    up = jnp.dot(h, wu_ref[...], preferred_element_type=F32)
    ff = (gate * jax.nn.sigmoid(gate)) * up
    o_ref[...] += jnp.dot(ff.astype(BF16), wd_ref[...], preferred_element_type=F32)

    if final_norm:
        @pl.when(j == pl.num_programs(1) - 1)
        def _():
            o_ref[...] = _rms(o_ref[...], fg_ref[...])


def _ffn(x, g, wg, wu, wd, fg, *, final_norm):
    t = x.shape[0]
    bm, bf = 512, 512
    return pl.pallas_call(
        functools.partial(_ffn_kernel, final_norm=final_norm),
        grid=(t // bm, D_FF // bf),
        in_specs=[
            pl.BlockSpec((bm, D_MODEL), lambda i, j: (i, 0)),
            pl.BlockSpec((1, D_MODEL), lambda i, j: (0, 0)),
            pl.BlockSpec((D_MODEL, bf), lambda i, j: (0, j)),
            pl.BlockSpec((D_MODEL, bf), lambda i, j: (0, j)),
            pl.BlockSpec((bf, D_MODEL), lambda i, j: (j, 0)),
            pl.BlockSpec((1, D_MODEL), lambda i, j: (0, 0)),
        ],
        out_specs=pl.BlockSpec((bm, D_MODEL), lambda i, j: (i, 0)),
        out_shape=jax.ShapeDtypeStruct((t, D_MODEL), F32),
        scratch_shapes=[pltpu.VMEM((bm, D_MODEL), BF16)],
        compiler_params=pltpu.CompilerParams(
            dimension_semantics=("parallel", "arbitrary"), vmem_limit_bytes=48 * MIB),
        name="swiglu_ffn",
    )(x, g, wg, wu, wd, fg)


def _rope_tables():
    pos = jnp.arange(SEQ, dtype=jnp.int32)

    def angles(p, dim):
        inv = ROPE_THETA ** (-jnp.arange(0, dim, 2, dtype=F32) / dim)
        return p.astype(F32)[:, None] * inv[None, :]

    ang_b = angles(pos, HEAD_DIM)
    ang_a = angles(pos, A_ROPE)
    ang_r = angles(pos // GRID_W, HEAD_DIM // 2)
    ang_c = angles(pos % GRID_W, HEAD_DIM // 2)
    z = jnp.zeros((SEQ, 32), F32)
    cat = lambda *xs: jnp.concatenate(xs, axis=-1)
    cos_a = cat(jnp.cos(ang_a), z, jnp.cos(ang_a), z)
    sin_a = cat(-jnp.sin(ang_a), z, jnp.sin(ang_a), z)
    cos_b = cat(jnp.cos(ang_b), jnp.cos(ang_b))
    sin_b = cat(-jnp.sin(ang_b), jnp.sin(ang_b))
    cos_c = cat(jnp.cos(ang_r), jnp.cos(ang_c), jnp.cos(ang_r), jnp.cos(ang_c))
    sin_c = cat(-jnp.sin(ang_r), -jnp.sin(ang_c), jnp.sin(ang_r), jnp.sin(ang_c))
    return cos_a, sin_a, cos_b, sin_b, cos_c, sin_c


def _axial_perm():
    a = jnp.arange(32)
    return jnp.concatenate([a, a + 64, a + 32, a + 96])


def _cast_kernel(w_ref, o_ref):
    o_ref[...] = w_ref[...].astype(BF16)


def _cast_layer_bf16(w, layer):
    _, r, c = w.shape
    br = 256
    return pl.pallas_call(
        _cast_kernel,
        grid=(r // br,),
        in_specs=[pl.BlockSpec((None, br, c), lambda i: (layer, i, 0))],
        out_specs=pl.BlockSpec((br, c), lambda i: (i, 0)),
        out_shape=jax.ShapeDtypeStruct((r, c), BF16),
        compiler_params=pltpu.CompilerParams(
            dimension_semantics=("parallel",), vmem_limit_bytes=40 * MIB),
        name="cast_bf16",
    )(w)


def _w_in_layout_kernel(w_ref, o_ref):
    br = w_ref.shape[0]
    o_ref[:, 0:OFF_KR] = w_ref[:, 0:OFF_KR].astype(BF16)
    rest = w_ref[:, OFF_KR:]
    z = jnp.zeros((br, 32), F32)
    o_ref[:, OFF_KR:OFF_BQ] = jnp.concatenate(
        [rest[:, 0:32], z, rest[:, 32:64], z], axis=1).astype(BF16)
    o_ref[:, OFF_BQ:OFF_CQH] = rest[:, 64:64 + 3 * B_WIDTH].astype(BF16)
    for h in range(C_HEADS + C_KV_HEADS):
        lo = 64 + 3 * B_WIDTH + h * 128
        x = rest[:, lo:lo + 128]
        o_ref[:, OFF_CQH + h * 128:OFF_CQH + (h + 1) * 128] = jnp.concatenate(
            [x[:, 0:32], x[:, 64:96], x[:, 32:64], x[:, 96:128]], axis=1).astype(BF16)
    lo = 64 + 3 * B_WIDTH + C_WIDTH + C_KV_WIDTH
    o_ref[:, OFF_CV:PROJ_WIDTH] = rest[:, lo:lo + C_KV_WIDTH].astype(BF16)


def _layout_w_in(w, layer):
    br = 256
    width = w.shape[2]
    return pl.pallas_call(
        _w_in_layout_kernel,
        grid=(D_MODEL // br,),
        in_specs=[pl.BlockSpec((None, br, width), lambda i: (layer, i, 0))],
        out_specs=pl.BlockSpec((br, PROJ_WIDTH), lambda i: (i, 0)),
        out_shape=jax.ShapeDtypeStruct((D_MODEL, PROJ_WIDTH), BF16),
        compiler_params=pltpu.CompilerParams(
            dimension_semantics=("parallel",), vmem_limit_bytes=40 * MIB),
        name="w_in_layout",
    )(w)


def _layout_w_uq(w):
    w = w.reshape(A_RANK, A_HEADS, A_NOPE + A_ROPE)
    z = jnp.zeros((A_RANK, A_HEADS, 32), w.dtype)
    out = jnp.concatenate([w[:, :, :128], w[:, :, 128:160], z, w[:, :, 160:192], z], axis=2)
    return out.reshape(A_RANK, A_HEADS * A_DK).astype(BF16)


def _layout_w_ukv(w):
    w = w.reshape(A_RANK, A_HEADS, 2 * HEAD_DIM)
    out = jnp.concatenate([w[:, :, :128].reshape(A_RANK, A_WIDTH),
                           w[:, :, 128:].reshape(A_RANK, A_WIDTH)], axis=1)
    return out.astype(BF16)


def kernel(x, attn_norm, w_in, a_q_norm, a_w_uq, a_kv_norm, a_w_ukv, c_q_norm, c_k_norm,
           out_norm, w_out, ffn_norm, w_gate, w_up, w_down, final_norm):
    bsz, seq, _ = x.shape
    t = bsz * seq
    depth = w_in.shape[0]
    tables = _rope_tables()
    perm = _axial_perm()
    xf = x.reshape(t, D_MODEL)
    for l in range(depth):
        proj = _in_proj(xf, attn_norm[l][None, :], _layout_w_in(w_in, l))
        (qa, ka, va, qb1, kb1, vb1, qb4, kb4, vb4, qb16, kb16, vb16, qc, kc, vc) = _prep(
            proj, tables, a_q_norm[l][None, :], a_kv_norm[l][None, :],
            c_q_norm[l][perm][None, :], c_k_norm[l][perm][None, :],
            _layout_w_uq(a_w_uq[l]), _layout_w_ukv(a_w_ukv[l]))
        r3 = lambda a, n=1: a.reshape(bsz, seq // n, a.shape[1])
        ya = _attention(r3(qa), r3(ka), r3(va), kv_heads=A_HEADS, group=1, dk=A_DK, dv=HEAD_DIM,
                        bq=512, name="mla_attn")
        yb = _dilated([r3(qb1), r3(kb1), r3(vb1), r3(qb4, 4), r3(kb4, 4), r3(vb4, 4),
                       r3(qb16, 16), r3(kb16, 16), r3(vb16, 16)])
        yc = _attention(r3(qc), r3(kc), r3(vc), kv_heads=C_KV_HEADS, group=C_GROUP, dk=HEAD_DIM,
                        dv=HEAD_DIM, bq=256, name="gqa_attn")
        xf = _out_proj(ya.reshape(t, A_WIDTH), yb.reshape(t, B_WIDTH), yc.reshape(t, C_WIDTH),
                       xf, out_norm[l][None, :], _cast_layer_bf16(w_out, l))
        xf = _ffn(xf, ffn_norm[l][None, :], _cast_layer_bf16(w_gate, l), _cast_layer_bf16(w_up, l),
                  _cast_layer_bf16(w_down, l), final_norm[None, :], final_norm=(l == depth - 1))
    return xf.reshape(bsz, seq, D_MODEL)
```

```python
import functools

import jax
import jax.numpy as jnp
from jax import lax
from jax.experimental import pallas as pl
from jax.experimental.pallas import tpu as pltpu

F32 = jnp.float32
BF16 = jnp.bfloat16

D_MODEL = 2048
SEQ = 4096
HEAD_DIM = 128
ROPE_THETA = 10000.0
GRID_W = 64
EPS = 1e-6
NEG = -1e30

A_HEADS = 4
A_RANK = 512
A_NOPE = 128
A_ROPE = 64
A_DK = 256
B_HEADS = 6
B_DILATIONS = (1, 4, 16)
B_HALF = 64
C_HEADS = 6
C_KV_HEADS = 2
C_GROUP = C_HEADS // C_KV_HEADS
A_WIDTH = A_HEADS * HEAD_DIM
B_WIDTH = B_HEADS * HEAD_DIM
C_WIDTH = C_HEADS * HEAD_DIM
C_KV_WIDTH = C_KV_HEADS * HEAD_DIM
D_FF = 5632

OFF_CQ = 0
OFF_CKV = 512
OFF_KR = 1024
OFF_BQ = 1152
OFF_BK = OFF_BQ + B_WIDTH
OFF_BV = OFF_BK + B_WIDTH
OFF_CQH = OFF_BV + B_WIDTH
OFF_CK = OFF_CQH + C_WIDTH
OFF_CV = OFF_CK + C_KV_WIDTH
PROJ_USED = OFF_CV + C_KV_WIDTH
PROJ_WIDTH = PROJ_USED

LOG2E = 1.4426950408889634
SCALE_A = (A_NOPE + A_ROPE) ** -0.5 * LOG2E
SCALE_B = HEAD_DIM ** -0.5 * LOG2E
SCALE_C = HEAD_DIM ** -0.5 * LOG2E

MIB = 1024 * 1024


def _rms(x, g):
    ms = jnp.mean(x * x, axis=-1, keepdims=True)
    return x * lax.rsqrt(ms + EPS) * g


def _rope(x, c, s):
    return x * c + pltpu.roll(x, 64, 1) * s


def _resident(shape):
    return pl.BlockSpec(shape, lambda i: (0,) * len(shape), pipeline_mode=pl.Buffered(1))


def _in_proj_kernel(x_ref, g_ref, w_ref, o_ref):
    h = _rms(x_ref[...], g_ref[...]).astype(BF16)
    o_ref[...] = jnp.dot(h, w_ref[...], preferred_element_type=F32)


def _in_proj(x, g, w):
    t = x.shape[0]
    bm = 512
    return pl.pallas_call(
        _in_proj_kernel,
        grid=(t // bm,),
        in_specs=[
            pl.BlockSpec((bm, D_MODEL), lambda i: (i, 0)),
            _resident((1, D_MODEL)),
            _resident((D_MODEL, PROJ_WIDTH)),
        ],
        out_specs=pl.BlockSpec((bm, PROJ_WIDTH), lambda i: (i, 0)),
        out_shape=jax.ShapeDtypeStruct((t, PROJ_WIDTH), F32),
        compiler_params=pltpu.CompilerParams(
            dimension_semantics=("parallel",), vmem_limit_bytes=56 * MIB),
        name="in_proj",
    )(x, g, w)


PREP_BM = 256


def _prep_kernel(p_ref, cos_a, sin_a, cos_b, sin_b, cos_c, sin_c, gq, gkv, gcq, gck, wuq, wukv,
                 sel_ref, qa, ka, va, qb1, kb1, vb1, qb4, kb4, vb4, qb16, kb16, vb16, qc, kc, vc):
    bm = PREP_BM
    ca, sa = cos_a[...], sin_a[...]
    cb, sb = cos_b[...], sin_b[...]
    cc, sc = cos_c[...], sin_c[...]

    cq = _rms(p_ref[:, OFF_CQ:OFF_CQ + A_RANK], gq[...]).astype(BF16)
    q = jnp.dot(cq, wuq[...], preferred_element_type=F32)
    for h in range(A_HEADS):
        lo = h * A_DK
        qa[:, lo:lo + 128] = (q[:, lo:lo + 128] * SCALE_A).astype(BF16)
        qa[:, lo + 128:lo + 256] = (_rope(q[:, lo + 128:lo + 256], ca, sa) * SCALE_A).astype(BF16)
    ckv = _rms(p_ref[:, OFF_CKV:OFF_CKV + A_RANK], gkv[...]).astype(BF16)
    kv = jnp.dot(ckv, wukv[...], preferred_element_type=F32)
    kr = _rope(p_ref[:, OFF_KR:OFF_KR + 128], ca, sa).astype(BF16)
    for h in range(A_HEADS):
        lo = h * A_DK
        ka[:, lo:lo + 128] = kv[:, h * 128:(h + 1) * 128].astype(BF16)
        ka[:, lo + 128:lo + 256] = kr
    va[...] = kv[:, A_WIDTH:2 * A_WIDTH].astype(BF16)

    sel = sel_ref[...]
    ones = jnp.ones((bm, 128), BF16)
    for off, scale, use_rope, is_v, o1, o4, o16 in (
            (OFF_BQ, SCALE_B, True, False, qb1, qb4, qb16),
            (OFF_BK, None, True, False, kb1, kb4, kb16),
            (OFF_BV, None, False, True, vb1, vb4, vb16)):
        width = 256 if is_v else 128
        tiles = []
        for h in range(B_HEADS):
            x = p_ref[:, off + h * 128:off + (h + 1) * 128]
            if use_rope:
                x = _rope(x, cb, sb)
            if scale is not None:
                x = x * scale
            tiles.append(x.astype(BF16))
            o1[:, h * width:h * width + 128] = tiles[h]
            if is_v:
                o1[:, h * width + 128:(h + 1) * width] = ones
        y = jnp.dot(sel, jnp.concatenate(tiles, axis=1), preferred_element_type=F32)
        for d, od, base in ((4, o4, 0), (16, o16, bm)):
            n = bm // d
            for h in range(B_HEADS):
                for r in range(d):
                    c0 = (h * d + r) * width
                    rows = y[base + r * n:base + (r + 1) * n, h * 128:(h + 1) * 128]
                    od[:, c0:c0 + 128] = rows.astype(BF16)
                    if is_v:
                        od[:, c0 + 128:c0 + 256] = ones[0:n]

    for h in range(C_HEADS):
        x = _rms(p_ref[:, OFF_CQH + h * 128:OFF_CQH + (h + 1) * 128], gcq[...])
        qc[:, h * 128:(h + 1) * 128] = (_rope(x, cc, sc) * SCALE_C).astype(BF16)
    for h in range(C_KV_HEADS):
        x = _rms(p_ref[:, OFF_CK + h * 128:OFF_CK + (h + 1) * 128], gck[...])
        kc[:, h * 128:(h + 1) * 128] = _rope(x, cc, sc).astype(BF16)
    vc[...] = p_ref[:, OFF_CV:OFF_CV + C_KV_WIDTH].astype(BF16)


def _prep(proj, tables, gq, gkv, gcq, gck, wuq, wukv):
    t = proj.shape[0]
    bm = PREP_BM
    nblk_seq = SEQ // bm
    row = lambda i: (i, 0)
    fixed = lambda i: (0, 0)
    tab = lambda i: (i % nblk_seq, 0)
    out_widths = [A_HEADS * A_DK, A_HEADS * A_DK, A_WIDTH]
    out_shapes = [jax.ShapeDtypeStruct((t, w), BF16) for w in out_widths]
    out_specs = [pl.BlockSpec((bm, w), row) for w in out_widths]
    for d in B_DILATIONS:
        for w in (B_WIDTH, B_WIDTH, 2 * B_WIDTH):
            out_shapes.append(jax.ShapeDtypeStruct((t // d, d * w), BF16))
            out_specs.append(pl.BlockSpec((bm // d, d * w), row))
    for w in (C_WIDTH, C_KV_WIDTH, C_KV_WIDTH):
        out_shapes.append(jax.ShapeDtypeStruct((t, w), BF16))
        out_specs.append(pl.BlockSpec((bm, w), row))
    in_specs = [pl.BlockSpec((bm, PROJ_WIDTH), row)]
    in_specs += [pl.BlockSpec((bm, 128), tab)] * 6
    in_specs += [pl.BlockSpec((1, A_RANK), fixed), pl.BlockSpec((1, A_RANK), fixed),
                 pl.BlockSpec((1, 128), fixed), pl.BlockSpec((1, 128), fixed),
                 pl.BlockSpec((A_RANK, A_HEADS * A_DK), fixed),
                 pl.BlockSpec((A_RANK, 2 * A_WIDTH), fixed),
                 pl.BlockSpec((2 * bm, bm), fixed)]
    return pl.pallas_call(
        _prep_kernel,
        grid=(t // bm,),
        in_specs=in_specs,
        out_specs=out_specs,
        out_shape=out_shapes,
        compiler_params=pltpu.CompilerParams(
            dimension_semantics=("parallel",), vmem_limit_bytes=40 * MIB),
        name="mixer_prep",
    )(proj, *tables, gq, gkv, gcq, gck, wuq, wukv, _stream_select(bm))


def _stream_select(bm):
    blocks = []
    for d in B_DILATIONS[1:]:
        out_row = jnp.arange(bm)
        src = (out_row % (bm // d)) * d + out_row // (bm // d)
        blocks.append(jax.nn.one_hot(src, bm, dtype=BF16))
    return jnp.concatenate(blocks, axis=0)


ATTN_KC = 1024
ATTN_SUB = 256
ATTN_PAD = 16


def _attn_kernel(q_ref, k_ref, v_ref, o_ref, vt_scr, *, group, dk, dv, bq, ones_row):
    @pl.when(pl.program_id(2) == 0)
    def _():
        vt_scr[0:dv, :] = v_ref[0].T
        if ones_row:
            first = lax.broadcasted_iota(jnp.int32, (ATTN_PAD, SEQ), 0) == 0
            vt_scr[dv:dv + ATTN_PAD, :] = jnp.where(first, 1.0, 0.0).astype(BF16)

    q_all = jnp.concatenate([q_ref[0, :, g * dk:(g + 1) * dk] for g in range(group)], axis=0)
    n_chunks = SEQ // ATTN_KC
    n_sub = ATTN_KC // ATTN_SUB

    def scores(c, j):
        r0 = c * ATTN_KC + j * ATTN_SUB
        return lax.dot_general(k_ref[0, r0:r0 + ATTN_SUB, :], q_all, (((1,), (1,)), ((), ())),
                               preferred_element_type=F32)

    def fold8(x, op):
        return op(x.reshape(x.shape[0] // 8, 8, x.shape[1]), axis=0)

    def col_max(pieces):
        mx = fold8(pieces[0], jnp.max)
        for piece in pieces[1:]:
            mx = jnp.maximum(mx, fold8(piece, jnp.max))
        return jnp.max(mx, axis=0, keepdims=True)

    s_cur = [scores(0, j) for j in range(n_sub)]
    m = l = acc = None
    for c in range(n_chunks):
        m_c = col_max(s_cur)
        m_new = m_c if c == 0 else jnp.maximum(m, m_c)
        s_next, p_sum, p_bf = [], None, []
        for j in range(n_sub):
            if c + 1 < n_chunks:
                s_next.append(scores(c + 1, j))
            p = jnp.exp2(s_cur[j] - m_new)
            if not ones_row:
                p_sum = fold8(p, jnp.sum) if j == 0 else p_sum + fold8(p, jnp.sum)
            p_bf.append(p.astype(BF16))
        pv = jnp.dot(vt_scr[:, c * ATTN_KC:(c + 1) * ATTN_KC], jnp.concatenate(p_bf, axis=0),
                     preferred_element_type=F32)
        l_c = None if ones_row else jnp.sum(p_sum, axis=0, keepdims=True)
        if c == 0:
            l, acc = l_c, pv
        else:
            alpha = jnp.exp2(m - m_new)
            l = None if ones_row else alpha * l + l_c
            acc = alpha * acc + pv
        m = m_new
        s_cur = s_next
    if ones_row:
        l = acc[dv:dv + 1, :]
    out = acc[0:dv, :] / l
    for g in range(group):
        o_ref[0, :, g * dv:(g + 1) * dv] = out[:, g * bq:(g + 1) * bq].T


def _attention(q, k, v, *, kv_heads, group, dk, dv, bq, ones_row, name):
    b = q.shape[0]
    vt_rows = dv + ATTN_PAD if ones_row else dv
    return pl.pallas_call(
        functools.partial(_attn_kernel, group=group, dk=dk, dv=dv, bq=bq, ones_row=ones_row),
        grid=(b, kv_heads, SEQ // bq),
        in_specs=[
            pl.BlockSpec((1, bq, group * dk), lambda bi, h, qi: (bi, qi, h)),
            pl.BlockSpec((1, SEQ, dk), lambda bi, h, qi: (bi, 0, h)),
            pl.BlockSpec((1, SEQ, dv), lambda bi, h, qi: (bi, 0, h)),
        ],
        out_specs=pl.BlockSpec((1, bq, group * dv), lambda bi, h, qi: (bi, qi, h)),
        out_shape=jax.ShapeDtypeStruct((b, SEQ, kv_heads * group * dv), F32),
        scratch_shapes=[pltpu.VMEM((vt_rows, SEQ), BF16)],
        compiler_params=pltpu.CompilerParams(
            dimension_semantics=("parallel", "parallel", "arbitrary"),
            vmem_limit_bytes=48 * MIB),
        name=name,
    )(q, k, v)


DIL_BQ = 128
DIL_WIN = DIL_BQ + 2 * B_HALF


def _dil_block(q_ref, k_ref, v_ref, bias_ref, stream, q0, length):
    if isinstance(q0, int):
        start = min(max(q0 - B_HALF, 0), length - DIL_WIN)
        case = (q0 - start) // B_HALF
    else:
        start = pl.multiple_of(jnp.clip(q0 - B_HALF, 0, length - DIL_WIN), B_HALF)
        case = lax.shift_right_logical(q0 - start, 6)
    q = q_ref[0, pl.ds(q0, DIL_BQ), stream * 128:(stream + 1) * 128]
    k = k_ref[0, pl.ds(start, DIL_WIN), stream * 128:(stream + 1) * 128]
    vx = v_ref[0, pl.ds(start, DIL_WIN), stream * 256:(stream + 1) * 256]
    s = lax.dot_general(q, k, (((1,), (1,)), ((), ())), preferred_element_type=F32)
    s = s + bias_ref[case]
    m = jnp.max(s, axis=-1, keepdims=True)
    p = jnp.exp2(s - m).astype(BF16)
    ol = jnp.dot(p, vx, preferred_element_type=F32)
    l = ol[:, 128:256]
    return ol[:, 0:128] / l, m + jnp.log(l) * LOG2E


def _dil_kernel(q1, k1, v1, q4, k4, v4, q16, k16, v16, bias_ref, o_ref,
                osm4, lsm4, osm16, lsm16, otok, ltok):
    def body1(n, carry):
        q0 = pl.multiple_of(n * DIL_BQ, DIL_BQ)
        o, lse = _dil_block(q1, k1, v1, bias_ref, 0, q0, SEQ)
        otok[0, pl.ds(q0, DIL_BQ), :] = o
        ltok[0, pl.ds(q0, DIL_BQ), :] = lse
        return carry

    lax.fori_loop(0, SEQ // DIL_BQ, body1, 0, unroll=8)

    def body4(n, carry):
        q0 = pl.multiple_of(n * DIL_BQ, DIL_BQ)
        for r in range(4):
            o, lse = _dil_block(q4, k4, v4, bias_ref, r, q0, SEQ // 4)
            osm4[pl.ds(q0, DIL_BQ), r * 128:(r + 1) * 128] = o
            lsm4[pl.ds(q0, DIL_BQ), r * 128:(r + 1) * 128] = lse
        return carry

    lax.fori_loop(0, SEQ // 4 // DIL_BQ, body4, 0, unroll=2)
    for r in range(4):
        otok[1, pl.ds(r, SEQ // 4, stride=4), :] = osm4[:, r * 128:(r + 1) * 128]
        ltok[1, pl.ds(r, SEQ // 4, stride=4), :] = lsm4[:, r * 128:(r + 1) * 128]

    for n in range(SEQ // 16 // DIL_BQ):
        q0 = n * DIL_BQ
        for r in range(16):
            o, lse = _dil_block(q16, k16, v16, bias_ref, r, q0, SEQ // 16)
            osm16[q0:q0 + DIL_BQ, r * 128:(r + 1) * 128] = o
            lsm16[q0:q0 + DIL_BQ, r * 128:(r + 1) * 128] = lse
    for r in range(16):
        otok[2, pl.ds(r, SEQ // 16, stride=16), :] = osm16[:, r * 128:(r + 1) * 128]
        ltok[2, pl.ds(r, SEQ // 16, stride=16), :] = lsm16[:, r * 128:(r + 1) * 128]

    chunk = 256

    def combine(c, carry):
        r0 = pl.multiple_of(c * chunk, chunk)
        l0 = ltok[0, pl.ds(r0, chunk), :]
        l1 = ltok[1, pl.ds(r0, chunk), :]
        l2 = ltok[2, pl.ds(r0, chunk), :]
        mx = jnp.maximum(jnp.maximum(l0, l1), l2)
        e0 = jnp.exp2(l0 - mx)
        e1 = jnp.exp2(l1 - mx)
        e2 = jnp.exp2(l2 - mx)
        num = (e0 * otok[0, pl.ds(r0, chunk), :] + e1 * otok[1, pl.ds(r0, chunk), :]
               + e2 * otok[2, pl.ds(r0, chunk), :])
        o_ref[0, pl.ds(r0, chunk), :] = num / (e0 + e1 + e2)
        return carry

    lax.fori_loop(0, SEQ // chunk, combine, 0)


def _dil_bias():
    i = jnp.arange(DIL_BQ)[:, None]
    j = jnp.arange(DIL_WIN)[None, :]
    return jnp.stack([jnp.where(jnp.abs(j - i - c * B_HALF) <= B_HALF, 0.0, NEG)
                      for c in range(3)]).astype(F32)


def _dilated(qkv_by_dilation):
    b = qkv_by_dilation[0].shape[0]
    in_specs = []
    for d in B_DILATIONS:
        in_specs += [pl.BlockSpec((1, SEQ // d, d * 128), lambda bi, h: (bi, 0, h))] * 2
        in_specs += [pl.BlockSpec((1, SEQ // d, d * 256), lambda bi, h: (bi, 0, h))]
    in_specs += [pl.BlockSpec((3, DIL_BQ, DIL_WIN), lambda bi, h: (0, 0, 0))]
    qkv_by_dilation = list(qkv_by_dilation) + [_dil_bias()]
    return pl.pallas_call(
        _dil_kernel,
        grid=(b, B_HEADS),
        in_specs=in_specs,
        out_specs=pl.BlockSpec((1, SEQ, 128), lambda bi, h: (bi, 0, h)),
        out_shape=jax.ShapeDtypeStruct((b, SEQ, B_WIDTH), F32),
        scratch_shapes=[
            pltpu.VMEM((SEQ // 4, 4 * 128), F32), pltpu.VMEM((SEQ // 4, 4 * 128), F32),
            pltpu.VMEM((SEQ // 16, 16 * 128), F32), pltpu.VMEM((SEQ // 16, 16 * 128), F32),
            pltpu.VMEM((3, SEQ, 128), F32), pltpu.VMEM((3, SEQ, 128), F32),
        ],
        compiler_params=pltpu.CompilerParams(
            dimension_semantics=("parallel", "parallel"), vmem_limit_bytes=56 * MIB),
        name="dilated_attn",
    )(*qkv_by_dilation)


def _out_proj_kernel(ya, yb, yc, x_ref, g_ref, w_ref, o_ref):
    b0, c0 = A_WIDTH, A_WIDTH + B_WIDTH
    na = _rms(ya[...], g_ref[:, 0:b0]).astype(BF16)
    nb = _rms(yb[...], g_ref[:, b0:c0]).astype(BF16)
    nc = _rms(yc[...], g_ref[:, c0:]).astype(BF16)
    y = (jnp.dot(na, w_ref[0:b0, :], preferred_element_type=F32)
         + jnp.dot(nb, w_ref[b0:c0, :], preferred_element_type=F32)
         + jnp.dot(nc, w_ref[c0:, :], preferred_element_type=F32))
    o_ref[...] = x_ref[...] + y


def _out_proj(ya, yb, yc, x, g, w):
    t = x.shape[0]
    bm = 512
    row = lambda i: (i, 0)
    return pl.pallas_call(
        _out_proj_kernel,
        grid=(t // bm,),
        in_specs=[
            pl.BlockSpec((bm, A_WIDTH), row),
            pl.BlockSpec((bm, B_WIDTH), row),
            pl.BlockSpec((bm, C_WIDTH), row),
            pl.BlockSpec((bm, D_MODEL), row),
            _resident((1, D_MODEL)),
            _resident((D_MODEL, D_MODEL)),
        ],
        out_specs=pl.BlockSpec((bm, D_MODEL), row),
        out_shape=jax.ShapeDtypeStruct((t, D_MODEL), F32),
        compiler_params=pltpu.CompilerParams(
            dimension_semantics=("parallel",), vmem_limit_bytes=48 * MIB),
        name="out_proj",
    )(ya, yb, yc, x, g, w)


FFN_BF = 512


def _ffn_kernel(x_ref, g_ref, wgu_ref, wd_ref, fg_ref, o_ref, h_scr, *, final_norm):
    j = pl.program_id(1)

    @pl.when(j == 0)
    def _():
        x = x_ref[...]
        h_scr[...] = _rms(x, g_ref[...]).astype(BF16)
        o_ref[...] = x

    gu = jnp.dot(h_scr[...], wgu_ref[...], preferred_element_type=F32)
    gate = gu[:, 0:FFN_BF]
    up = gu[:, FFN_BF:2 * FFN_BF]
    ff = (gate * jax.nn.sigmoid(gate)) * up
    o_ref[...] += jnp.dot(ff.astype(BF16), wd_ref[...], preferred_element_type=F32)

    if final_norm:
        @pl.when(j == pl.num_programs(1) - 1)
        def _():
            o_ref[...] = _rms(o_ref[...], fg_ref[...])


def _ffn(x, g, wgu, wd, fg, *, final_norm, bm):
    t = x.shape[0]
    bf = FFN_BF
    return pl.pallas_call(
        functools.partial(_ffn_kernel, final_norm=final_norm),
        grid=(t // bm, D_FF // bf),
        in_specs=[
            pl.BlockSpec((bm, D_MODEL), lambda i, j: (i, 0)),
            pl.BlockSpec((1, D_MODEL), lambda i, j: (0, 0)),
            pl.BlockSpec((D_MODEL, 2 * bf), lambda i, j: (0, j)),
            pl.BlockSpec((bf, D_MODEL), lambda i, j: (j, 0)),
            pl.BlockSpec((1, D_MODEL), lambda i, j: (0, 0)),
        ],
        out_specs=pl.BlockSpec((bm, D_MODEL), lambda i, j: (i, 0)),
        out_shape=jax.ShapeDtypeStruct((t, D_MODEL), F32),
        scratch_shapes=[pltpu.VMEM((bm, D_MODEL), BF16)],
        compiler_params=pltpu.CompilerParams(
            dimension_semantics=("parallel", "arbitrary"), vmem_limit_bytes=60 * MIB),
        name="swiglu_ffn",
    )(x, g, wgu, wd, fg)


def _rope_tables():
    pos = jnp.arange(SEQ, dtype=jnp.int32)

    def angles(p, dim):
        inv = ROPE_THETA ** (-jnp.arange(0, dim, 2, dtype=F32) / dim)
        return p.astype(F32)[:, None] * inv[None, :]

    ang_b = angles(pos, HEAD_DIM)
    ang_a = angles(pos, A_ROPE)
    ang_r = angles(pos // GRID_W, HEAD_DIM // 2)
    ang_c = angles(pos % GRID_W, HEAD_DIM // 2)
    z = jnp.zeros((SEQ, 32), F32)
    cat = lambda *xs: jnp.concatenate(xs, axis=-1)
    cos_a = cat(jnp.cos(ang_a), z, jnp.cos(ang_a), z)
    sin_a = cat(-jnp.sin(ang_a), z, jnp.sin(ang_a), z)
    cos_b = cat(jnp.cos(ang_b), jnp.cos(ang_b))
    sin_b = cat(-jnp.sin(ang_b), jnp.sin(ang_b))
    cos_c = cat(jnp.cos(ang_r), jnp.cos(ang_c), jnp.cos(ang_r), jnp.cos(ang_c))
    sin_c = cat(-jnp.sin(ang_r), -jnp.sin(ang_c), jnp.sin(ang_r), jnp.sin(ang_c))
    return cos_a, sin_a, cos_b, sin_b, cos_c, sin_c


def _axial_perm():
    a = jnp.arange(32)
    return jnp.concatenate([a, a + 64, a + 32, a + 96])


def _cast_kernel(w_ref, o_ref):
    o_ref[...] = w_ref[...].astype(BF16)


def _cast_layer_bf16(w, layer):
    _, r, c = w.shape
    br = 256
    return pl.pallas_call(
        _cast_kernel,
        grid=(r // br,),
        in_specs=[pl.BlockSpec((None, br, c), lambda i: (layer, i, 0))],
        out_specs=pl.BlockSpec((br, c), lambda i: (i, 0)),
        out_shape=jax.ShapeDtypeStruct((r, c), BF16),
        compiler_params=pltpu.CompilerParams(
            dimension_semantics=("parallel",), vmem_limit_bytes=40 * MIB),
        name="cast_bf16",
    )(w)


def _cast_pair_kernel(a_ref, b_ref, o_ref):
    n = a_ref.shape[1]
    o_ref[:, 0:n] = a_ref[...].astype(BF16)
    o_ref[:, n:2 * n] = b_ref[...].astype(BF16)


def _cast_gate_up(w_gate, w_up, layer):
    br = 512
    return pl.pallas_call(
        _cast_pair_kernel,
        grid=(D_MODEL // br, D_FF // FFN_BF),
        in_specs=[pl.BlockSpec((None, br, FFN_BF), lambda i, j: (layer, i, j)),
                  pl.BlockSpec((None, br, FFN_BF), lambda i, j: (layer, i, j))],
        out_specs=pl.BlockSpec((br, 2 * FFN_BF), lambda i, j: (i, j)),
        out_shape=jax.ShapeDtypeStruct((D_MODEL, 2 * D_FF), BF16),
        compiler_params=pltpu.CompilerParams(
            dimension_semantics=("parallel", "parallel"), vmem_limit_bytes=40 * MIB),
        name="cast_gate_up",
    )(w_gate, w_up)


def _w_in_layout_kernel(w_ref, o_ref):
    bl = w_ref.shape[1]

    def put(col, rows):
        o_ref[:, col:col + 128] = rows.T.astype(BF16)

    for t in range(OFF_KR // 128):
        put(t * 128, w_ref[t * 128:(t + 1) * 128, :])
    z = jnp.zeros((32, bl), F32)
    put(OFF_KR, jnp.concatenate([w_ref[1024:1056, :], z, w_ref[1056:1088, :], z], axis=0))
    for t in range(3 * B_WIDTH // 128):
        put(OFF_BQ + t * 128, w_ref[1088 + t * 128:1088 + (t + 1) * 128, :])
    for h in range(C_HEADS + C_KV_HEADS):
        r0 = 3392 + h * 128
        put(OFF_CQH + h * 128, jnp.concatenate(
            [w_ref[r0:r0 + 32, :], w_ref[r0 + 64:r0 + 96, :],
             w_ref[r0 + 32:r0 + 64, :], w_ref[r0 + 96:r0 + 128, :]], axis=0))
    for t in range(C_KV_WIDTH // 128):
        put(OFF_CV + t * 128, w_ref[4416 + t * 128:4416 + (t + 1) * 128, :])


def _layout_w_in(w, layer):
    wt = jnp.swapaxes(w, 1, 2)
    bl = 256
    width = wt.shape[1]
    return pl.pallas_call(
        _w_in_layout_kernel,
        grid=(D_MODEL // bl,),
        in_specs=[pl.BlockSpec((None, width, bl), lambda i: (layer, 0, i))],
        out_specs=pl.BlockSpec((bl, PROJ_WIDTH), lambda i: (i, 0)),
        out_shape=jax.ShapeDtypeStruct((D_MODEL, PROJ_WIDTH), BF16),
        compiler_params=pltpu.CompilerParams(
            dimension_semantics=("parallel",), vmem_limit_bytes=40 * MIB),
        name="w_in_layout",
    )(wt)


def _layout_w_uq(w):
    w = w.reshape(A_RANK, A_HEADS, A_NOPE + A_ROPE)
    z = jnp.zeros((A_RANK, A_HEADS, 32), w.dtype)
    out = jnp.concatenate([w[:, :, :128], w[:, :, 128:160], z, w[:, :, 160:192], z], axis=2)
    return out.reshape(A_RANK, A_HEADS * A_DK).astype(BF16)


def _layout_w_ukv(w):
    w = w.reshape(A_RANK, A_HEADS, 2 * HEAD_DIM)
    out = jnp.concatenate([w[:, :, :128].reshape(A_RANK, A_WIDTH),
                           w[:, :, 128:].reshape(A_RANK, A_WIDTH)], axis=1)
    return out.astype(BF16)


def kernel(x, attn_norm, w_in, a_q_norm, a_w_uq, a_kv_norm, a_w_ukv, c_q_norm, c_k_norm,
           out_norm, w_out, ffn_norm, w_gate, w_up, w_down, final_norm):
    bsz, seq, _ = x.shape
    t = bsz * seq
    depth = w_in.shape[0]
    tables = _rope_tables()
    perm = _axial_perm()
    xf = x.reshape(t, D_MODEL)
    for l in range(depth):
        proj = _in_proj(xf, attn_norm[l][None, :], _layout_w_in(w_in, l))
        (qa, ka, va, qb1, kb1, vb1, qb4, kb4, vb4, qb16, kb16, vb16, qc, kc, vc) = _prep(
            proj, tables, a_q_norm[l][None, :], a_kv_norm[l][None, :],
            c_q_norm[l][perm][None, :], c_k_norm[l][perm][None, :],
            _layout_w_uq(a_w_uq[l]), _layout_w_ukv(a_w_ukv[l]))
        r3 = lambda a, n=1: a.reshape(bsz, seq // n, a.shape[1])
        ya = _attention(r3(qa), r3(ka), r3(va), kv_heads=A_HEADS, group=1, dk=A_DK, dv=HEAD_DIM,
                        bq=512, ones_row=(l == 1), name="mla_attn")
        yb = _dilated([r3(qb1), r3(kb1), r3(vb1), r3(qb4, 4), r3(kb4, 4), r3(vb4, 4),
                       r3(qb16, 16), r3(kb16, 16), r3(vb16, 16)])
        yc = _attention(r3(qc), r3(kc), r3(vc), kv_heads=C_KV_HEADS, group=C_GROUP, dk=HEAD_DIM,
                        dv=HEAD_DIM, bq=256, ones_row=(l == 1), name="gqa_attn")
        xf = _out_proj(ya.reshape(t, A_WIDTH), yb.reshape(t, B_WIDTH), yc.reshape(t, C_WIDTH),
                       xf, out_norm[l][None, :], _cast_layer_bf16(w_out, l))
        xf = _ffn(xf, ffn_norm[l][None, :], _cast_gate_up(w_gate, w_up, l),
                  _cast_layer_bf16(w_down, l), final_norm[None, :], final_norm=(l == depth - 1),
                  bm=512 if l == 0 else 1024)
    return xf.reshape(bsz, seq, D_MODEL)
```

```python
import functools

import jax
import jax.numpy as jnp
from jax import lax
from jax.experimental import pallas as pl
from jax.experimental.pallas import tpu as pltpu

F32 = jnp.float32
BF16 = jnp.bfloat16

D_MODEL = 2048
SEQ = 4096
HEAD_DIM = 128
ROPE_THETA = 10000.0
GRID_W = 64
EPS = 1e-6
NEG = -1e30

A_HEADS = 4
A_RANK = 512
A_NOPE = 128
A_ROPE = 64
A_DK = 256
B_HEADS = 6
B_DILATIONS = (1, 4, 16)
B_HALF = 64
C_HEADS = 6
C_KV_HEADS = 2
C_GROUP = C_HEADS // C_KV_HEADS
A_WIDTH = A_HEADS * HEAD_DIM
B_WIDTH = B_HEADS * HEAD_DIM
C_WIDTH = C_HEADS * HEAD_DIM
C_KV_WIDTH = C_KV_HEADS * HEAD_DIM
D_FF = 5632

OFF_CQ = 0
OFF_CKV = 512
OFF_KR = 1024
OFF_BQ = 1152
OFF_BK = OFF_BQ + B_WIDTH
OFF_BV = OFF_BK + B_WIDTH
OFF_CQH = OFF_BV + B_WIDTH
OFF_CK = OFF_CQH + C_WIDTH
OFF_CV = OFF_CK + C_KV_WIDTH
PROJ_USED = OFF_CV + C_KV_WIDTH
PROJ_WIDTH = PROJ_USED

LOG2E = 1.4426950408889634
SCALE_A = (A_NOPE + A_ROPE) ** -0.5 * LOG2E
SCALE_B = HEAD_DIM ** -0.5 * LOG2E
SCALE_C = HEAD_DIM ** -0.5 * LOG2E

MIB = 1024 * 1024


def _rms(x, g):
    ms = jnp.mean(x * x, axis=-1, keepdims=True)
    return x * lax.rsqrt(ms + EPS) * g


def _rope(x, c, s):
    return x * c + pltpu.roll(x, 64, 1) * s


def _resident(shape):
    return pl.BlockSpec(shape, lambda i: (0,) * len(shape), pipeline_mode=pl.Buffered(1))


PREP_BM = 256


def _proj_prep_kernel(x_ref, g_ref, w_ref, cos_a, sin_a, cos_b, sin_b, cos_c, sin_c,
                      gq, gkv, gcq, gck, wuq, wukv, sel_ref,
                      qa, ka, va, qb1, kb1, vb1, qb4, kb4, vb4, qb16, kb16, vb16, qc, kc, vc):
    bm = PREP_BM
    hn = _rms(x_ref[...], g_ref[...]).astype(BF16)

    def project(lo, hi):
        return jnp.dot(hn, w_ref[:, lo:hi], preferred_element_type=F32)

    seg_a = project(0, OFF_BQ)
    seg_bq = project(OFF_BQ, OFF_BK)

    ca, sa = cos_a[...], sin_a[...]
    cq = _rms(seg_a[:, OFF_CQ:OFF_CQ + A_RANK], gq[...]).astype(BF16)
    q = jnp.dot(cq, wuq[...], preferred_element_type=F32)
    for h in range(A_HEADS):
        lo = h * A_DK
        qa[:, lo:lo + 128] = (q[:, lo:lo + 128] * SCALE_A).astype(BF16)
        qa[:, lo + 128:lo + 256] = (_rope(q[:, lo + 128:lo + 256], ca, sa) * SCALE_A).astype(BF16)
    ckv = _rms(seg_a[:, OFF_CKV:OFF_CKV + A_RANK], gkv[...]).astype(BF16)
    kv = jnp.dot(ckv, wukv[...], preferred_element_type=F32)
    kr = _rope(seg_a[:, OFF_KR:OFF_KR + 128], ca, sa).astype(BF16)
    for h in range(A_HEADS):
        lo = h * A_DK
        ka[:, lo:lo + 128] = kv[:, h * 128:(h + 1) * 128].astype(BF16)
        ka[:, lo + 128:lo + 256] = kr
    va[...] = kv[:, A_WIDTH:2 * A_WIDTH].astype(BF16)

    cb, sb = cos_b[...], sin_b[...]
    sel = sel_ref[...]
    ones = jnp.ones((bm, 128), BF16)

    def mixer_b(seg, scale, use_rope, is_v, o1, o4, o16):
        width = 256 if is_v else 128
        tiles = []
        for h in range(B_HEADS):
            x = seg[:, h * 128:(h + 1) * 128]
            if use_rope:
                x = _rope(x, cb, sb)
            if scale is not None:
                x = x * scale
            tiles.append(x.astype(BF16))
            o1[:, h * width:h * width + 128] = tiles[h]
            if is_v:
                o1[:, h * width + 128:(h + 1) * width] = ones
        y = jnp.dot(sel, jnp.concatenate(tiles, axis=1), preferred_element_type=F32)
        for d, od, base in ((4, o4, 0), (16, o16, bm)):
            n = bm // d
            for h in range(B_HEADS):
                for r in range(d):
                    c0 = (h * d + r) * width
                    rows = y[base + r * n:base + (r + 1) * n, h * 128:(h + 1) * 128]
                    od[:, c0:c0 + 128] = rows.astype(BF16)
                    if is_v:
                        od[:, c0 + 128:c0 + 256] = ones[0:n]

    seg_bk = project(OFF_BK, OFF_BV)
    mixer_b(seg_bq, SCALE_B, True, False, qb1, qb4, qb16)
    seg_bv = project(OFF_BV, OFF_CQH)
    mixer_b(seg_bk, None, True, False, kb1, kb4, kb16)
    seg_c = project(OFF_CQH, PROJ_WIDTH)
    mixer_b(seg_bv, None, False, True, vb1, vb4, vb16)

    cc, sc = cos_c[...], sin_c[...]
    for h in range(C_HEADS):
        x = _rms(seg_c[:, h * 128:(h + 1) * 128], gcq[...])
        qc[:, h * 128:(h + 1) * 128] = (_rope(x, cc, sc) * SCALE_C).astype(BF16)
    for h in range(C_KV_HEADS):
        lo = C_WIDTH + h * 128
        x = _rms(seg_c[:, lo:lo + 128], gck[...])
        kc[:, h * 128:(h + 1) * 128] = _rope(x, cc, sc).astype(BF16)
    vc[...] = seg_c[:, C_WIDTH + C_KV_WIDTH:].astype(BF16)


def _proj_prep(x, g, w, tables, gq, gkv, gcq, gck, wuq, wukv):
    t = x.shape[0]
    bm = PREP_BM
    nblk_seq = SEQ // bm
    row = lambda i: (i, 0)
    tab = lambda i: (i % nblk_seq, 0)
    out_widths = [A_HEADS * A_DK, A_HEADS * A_DK, A_WIDTH]
    out_shapes = [jax.ShapeDtypeStruct((t, w_), BF16) for w_ in out_widths]
    out_specs = [pl.BlockSpec((bm, w_), row) for w_ in out_widths]
    for d in B_DILATIONS:
        for w_ in (B_WIDTH, B_WIDTH, 2 * B_WIDTH):
            out_shapes.append(jax.ShapeDtypeStruct((t // d, d * w_), BF16))
            out_specs.append(pl.BlockSpec((bm // d, d * w_), row))
    for w_ in (C_WIDTH, C_KV_WIDTH, C_KV_WIDTH):
        out_shapes.append(jax.ShapeDtypeStruct((t, w_), BF16))
        out_specs.append(pl.BlockSpec((bm, w_), row))
    in_specs = [pl.BlockSpec((bm, D_MODEL), row), _resident((1, D_MODEL)),
                _resident((D_MODEL, PROJ_WIDTH))]
    in_specs += [pl.BlockSpec((bm, 128), tab)] * 6
    in_specs += [_resident((1, A_RANK)), _resident((1, A_RANK)),
                 _resident((1, 128)), _resident((1, 128)),
                 _resident((A_RANK, A_HEADS * A_DK)), _resident((A_RANK, 2 * A_WIDTH)),
                 _resident((2 * bm, bm))]
    return pl.pallas_call(
        _proj_prep_kernel,
        grid=(t // bm,),
        in_specs=in_specs,
        out_specs=out_specs,
        out_shape=out_shapes,
        compiler_params=pltpu.CompilerParams(
            dimension_semantics=("parallel",), vmem_limit_bytes=56 * MIB),
        name="proj_prep",
    )(x, g, w, *tables, gq, gkv, gcq, gck, wuq, wukv, _stream_select(bm))


def _stream_select(bm):
    blocks = []
    for d in B_DILATIONS[1:]:
        out_row = jnp.arange(bm)
        src = (out_row % (bm // d)) * d + out_row // (bm // d)
        blocks.append(jax.nn.one_hot(src, bm, dtype=BF16))
    return jnp.concatenate(blocks, axis=0)


ATTN_KC = 1024
ATTN_SUB = 256
ATTN_PAD = 16


def _attn_kernel(q_ref, k_ref, v_ref, o_ref, vt_scr, *, group, dk, dv, bq, ones_row):
    @pl.when(pl.program_id(2) == 0)
    def _():
        vt_scr[0:dv, :] = v_ref[0].T
        if ones_row:
            first = lax.broadcasted_iota(jnp.int32, (ATTN_PAD, SEQ), 0) == 0
            vt_scr[dv:dv + ATTN_PAD, :] = jnp.where(first, 1.0, 0.0).astype(BF16)

    q_all = jnp.concatenate([q_ref[0, :, g * dk:(g + 1) * dk] for g in range(group)], axis=0)
    n_chunks = SEQ // ATTN_KC
    n_sub = ATTN_KC // ATTN_SUB

    def scores(c, j):
        r0 = c * ATTN_KC + j * ATTN_SUB
        return lax.dot_general(k_ref[0, r0:r0 + ATTN_SUB, :], q_all, (((1,), (1,)), ((), ())),
                               preferred_element_type=F32)

    def fold8(x, op):
        return op(x.reshape(x.shape[0] // 8, 8, x.shape[1]), axis=0)

    def col_max(pieces):
        mx = fold8(pieces[0], jnp.max)
        for piece in pieces[1:]:
            mx = jnp.maximum(mx, fold8(piece, jnp.max))
        return jnp.max(mx, axis=0, keepdims=True)

    s_cur = [scores(0, j) for j in range(n_sub)]
    m = l = acc = None
    for c in range(n_chunks):
        m_c = col_max(s_cur)
        m_new = m_c if c == 0 else jnp.maximum(m, m_c)
        s_next, p_sum, p_bf = [], None, []
        for j in range(n_sub):
            if c + 1 < n_chunks:
                s_next.append(scores(c + 1, j))
            p = jnp.exp2(s_cur[j] - m_new)
            if not ones_row:
                p_sum = fold8(p, jnp.sum) if j == 0 else p_sum + fold8(p, jnp.sum)
            p_bf.append(p.astype(BF16))
        pv = jnp.dot(vt_scr[:, c * ATTN_KC:(c + 1) * ATTN_KC], jnp.concatenate(p_bf, axis=0),
                     preferred_element_type=F32)
        l_c = None if ones_row else jnp.sum(p_sum, axis=0, keepdims=True)
        if c == 0:
            l, acc = l_c, pv
        else:
            alpha = jnp.exp2(m - m_new)
            l = None if ones_row else alpha * l + l_c
            acc = alpha * acc + pv
        m = m_new
        s_cur = s_next
    if ones_row:
        l = acc[dv:dv + 1, :]
    out = acc[0:dv, :] / l
    for g in range(group):
        o_ref[0, :, g * dv:(g + 1) * dv] = out[:, g * bq:(g + 1) * bq].T


def _attention(q, k, v, *, kv_heads, group, dk, dv, bq, ones_row, name):
    b = q.shape[0]
    vt_rows = dv + ATTN_PAD if ones_row else dv
    return pl.pallas_call(
        functools.partial(_attn_kernel, group=group, dk=dk, dv=dv, bq=bq, ones_row=ones_row),
        grid=(b, kv_heads, SEQ // bq),
        in_specs=[
            pl.BlockSpec((1, bq, group * dk), lambda bi, h, qi: (bi, qi, h)),
            pl.BlockSpec((1, SEQ, dk), lambda bi, h, qi: (bi, 0, h)),
            pl.BlockSpec((1, SEQ, dv), lambda bi, h, qi: (bi, 0, h)),
        ],
        out_specs=pl.BlockSpec((1, bq, group * dv), lambda bi, h, qi: (bi, qi, h)),
        out_shape=jax.ShapeDtypeStruct((b, SEQ, kv_heads * group * dv), F32),
        scratch_shapes=[pltpu.VMEM((vt_rows, SEQ), BF16)],
        compiler_params=pltpu.CompilerParams(
            dimension_semantics=("parallel", "parallel", "arbitrary"),
            vmem_limit_bytes=48 * MIB),
        name=name,
    )(q, k, v)


DIL_BQ = 128
DIL_WIN = DIL_BQ + 2 * B_HALF


def _dil_block(q_ref, k_ref, v_ref, bias_ref, stream, q0, length):
    if isinstance(q0, int):
        start = min(max(q0 - B_HALF, 0), length - DIL_WIN)
        case = (q0 - start) // B_HALF
    else:
        start = pl.multiple_of(jnp.clip(q0 - B_HALF, 0, length - DIL_WIN), B_HALF)
        case = lax.shift_right_logical(q0 - start, 6)
    q = q_ref[0, pl.ds(q0, DIL_BQ), stream * 128:(stream + 1) * 128]
    k = k_ref[0, pl.ds(start, DIL_WIN), stream * 128:(stream + 1) * 128]
    vx = v_ref[0, pl.ds(start, DIL_WIN), stream * 256:(stream + 1) * 256]
    s = lax.dot_general(q, k, (((1,), (1,)), ((), ())), preferred_element_type=F32)
    s = s + bias_ref[case]
    m = jnp.max(s, axis=-1, keepdims=True)
    p = jnp.exp2(s - m).astype(BF16)
    ol = jnp.dot(p, vx, preferred_element_type=F32)
    l = ol[:, 128:256]
    return ol[:, 0:128] / l, m + jnp.log(l) * LOG2E


def _dil_kernel(q1, k1, v1, q4, k4, v4, q16, k16, v16, bias_ref, o_ref,
                osm4, lsm4, osm16, lsm16, otok, ltok):
    def body1(n, carry):
        q0 = pl.multiple_of(n * DIL_BQ, DIL_BQ)
        o, lse = _dil_block(q1, k1, v1, bias_ref, 0, q0, SEQ)
        otok[0, pl.ds(q0, DIL_BQ), :] = o
        ltok[0, pl.ds(q0, DIL_BQ), :] = lse
        return carry

    lax.fori_loop(0, SEQ // DIL_BQ, body1, 0, unroll=8)

    def body4(n, carry):
        q0 = pl.multiple_of(n * DIL_BQ, DIL_BQ)
        for r in range(4):
            o, lse = _dil_block(q4, k4, v4, bias_ref, r, q0, SEQ // 4)
            osm4[pl.ds(q0, DIL_BQ), r * 128:(r + 1) * 128] = o
            lsm4[pl.ds(q0, DIL_BQ), r * 128:(r + 1) * 128] = lse
        return carry

    lax.fori_loop(0, SEQ // 4 // DIL_BQ, body4, 0, unroll=2)
    for r in range(4):
        otok[1, pl.ds(r, SEQ // 4, stride=4), :] = osm4[:, r * 128:(r + 1) * 128]
        ltok[1, pl.ds(r, SEQ // 4, stride=4), :] = lsm4[:, r * 128:(r + 1) * 128]

    for n in range(SEQ // 16 // DIL_BQ):
        q0 = n * DIL_BQ
        for r in range(16):
            o, lse = _dil_block(q16, k16, v16, bias_ref, r, q0, SEQ // 16)
            osm16[q0:q0 + DIL_BQ, r * 128:(r + 1) * 128] = o
            lsm16[q0:q0 + DIL_BQ, r * 128:(r + 1) * 128] = lse
    for r in range(16):
        otok[2, pl.ds(r, SEQ // 16, stride=16), :] = osm16[:, r * 128:(r + 1) * 128]
        ltok[2, pl.ds(r, SEQ // 16, stride=16), :] = lsm16[:, r * 128:(r + 1) * 128]

    chunk = 256

    def combine(c, carry):
        r0 = pl.multiple_of(c * chunk, chunk)
        l0 = ltok[0, pl.ds(r0, chunk), :]
        l1 = ltok[1, pl.ds(r0, chunk), :]
        l2 = ltok[2, pl.ds(r0, chunk), :]
        mx = jnp.maximum(jnp.maximum(l0, l1), l2)
        e0 = jnp.exp2(l0 - mx)
        e1 = jnp.exp2(l1 - mx)
        e2 = jnp.exp2(l2 - mx)
        num = (e0 * otok[0, pl.ds(r0, chunk), :] + e1 * otok[1, pl.ds(r0, chunk), :]
               + e2 * otok[2, pl.ds(r0, chunk), :])
        o_ref[0, pl.ds(r0, chunk), :] = num / (e0 + e1 + e2)
        return carry

    lax.fori_loop(0, SEQ // chunk, combine, 0)


def _dil_bias():
    i = jnp.arange(DIL_BQ)[:, None]
    j = jnp.arange(DIL_WIN)[None, :]
    return jnp.stack([jnp.where(jnp.abs(j - i - c * B_HALF) <= B_HALF, 0.0, NEG)
                      for c in range(3)]).astype(F32)


def _dilated(qkv_by_dilation):
    b = qkv_by_dilation[0].shape[0]
    in_specs = []
    for d in B_DILATIONS:
        in_specs += [pl.BlockSpec((1, SEQ // d, d * 128), lambda bi, h: (bi, 0, h))] * 2
        in_specs += [pl.BlockSpec((1, SEQ // d, d * 256), lambda bi, h: (bi, 0, h))]
    in_specs += [pl.BlockSpec((3, DIL_BQ, DIL_WIN), lambda bi, h: (0, 0, 0))]
    qkv_by_dilation = list(qkv_by_dilation) + [_dil_bias()]
    return pl.pallas_call(
        _dil_kernel,
        grid=(b, B_HEADS),
        in_specs=in_specs,
        out_specs=pl.BlockSpec((1, SEQ, 128), lambda bi, h: (bi, 0, h)),
        out_shape=jax.ShapeDtypeStruct((b, SEQ, B_WIDTH), F32),
        scratch_shapes=[
            pltpu.VMEM((SEQ // 4, 4 * 128), F32), pltpu.VMEM((SEQ // 4, 4 * 128), F32),
            pltpu.VMEM((SEQ // 16, 16 * 128), F32), pltpu.VMEM((SEQ // 16, 16 * 128), F32),
            pltpu.VMEM((3, SEQ, 128), F32), pltpu.VMEM((3, SEQ, 128), F32),
        ],
        compiler_params=pltpu.CompilerParams(
            dimension_semantics=("parallel", "parallel"), vmem_limit_bytes=56 * MIB),
        name="dilated_attn",
    )(*qkv_by_dilation)


def _out_proj_kernel(ya, yb, yc, x_ref, g_ref, w_ref, o_ref):
    b0, c0 = A_WIDTH, A_WIDTH + B_WIDTH
    na = _rms(ya[...], g_ref[:, 0:b0]).astype(BF16)
    nb = _rms(yb[...], g_ref[:, b0:c0]).astype(BF16)
    nc = _rms(yc[...], g_ref[:, c0:]).astype(BF16)
    y = (jnp.dot(na, w_ref[0:b0, :], preferred_element_type=F32)
         + jnp.dot(nb, w_ref[b0:c0, :], preferred_element_type=F32)
         + jnp.dot(nc, w_ref[c0:, :], preferred_element_type=F32))
    o_ref[...] = x_ref[...] + y


def _out_proj(ya, yb, yc, x, g, w):
    t = x.shape[0]
    bm = 512
    row = lambda i: (i, 0)
    return pl.pallas_call(
        _out_proj_kernel,
        grid=(t // bm,),
        in_specs=[
            pl.BlockSpec((bm, A_WIDTH), row),
            pl.BlockSpec((bm, B_WIDTH), row),
            pl.BlockSpec((bm, C_WIDTH), row),
            pl.BlockSpec((bm, D_MODEL), row),
            _resident((1, D_MODEL)),
            _resident((D_MODEL, D_MODEL)),
        ],
        out_specs=pl.BlockSpec((bm, D_MODEL), row),
        out_shape=jax.ShapeDtypeStruct((t, D_MODEL), F32),
        compiler_params=pltpu.CompilerParams(
            dimension_semantics=("parallel",), vmem_limit_bytes=48 * MIB),
        name="out_proj",
    )(ya, yb, yc, x, g, w)


FFN_BF = 512


def _ffn_kernel(x_ref, g_ref, wgu_ref, wd_ref, fg_ref, o_ref, h_scr, *, final_norm):
    j = pl.program_id(1)

    @pl.when(j == 0)
    def _():
        x = x_ref[...]
        h_scr[...] = _rms(x, g_ref[...]).astype(BF16)
        o_ref[...] = x

    gu = jnp.dot(h_scr[...], wgu_ref[...], preferred_element_type=F32)
    gate = gu[:, 0:FFN_BF]
    up = gu[:, FFN_BF:2 * FFN_BF]
    ff = (gate * jax.nn.sigmoid(gate)) * up
    o_ref[...] += jnp.dot(ff.astype(BF16), wd_ref[...], preferred_element_type=F32)

    if final_norm:
        @pl.when(j == pl.num_programs(1) - 1)
        def _():
            o_ref[...] = _rms(o_ref[...], fg_ref[...])


def _ffn(x, g, wgu, wd, fg, *, final_norm, bm):
    t = x.shape[0]
    bf = FFN_BF
    return pl.pallas_call(
        functools.partial(_ffn_kernel, final_norm=final_norm),
        grid=(t // bm, D_FF // bf),
        in_specs=[
            pl.BlockSpec((bm, D_MODEL), lambda i, j: (i, 0)),
            pl.BlockSpec((1, D_MODEL), lambda i, j: (0, 0)),
            pl.BlockSpec((D_MODEL, 2 * bf), lambda i, j: (0, j)),
            pl.BlockSpec((bf, D_MODEL), lambda i, j: (j, 0)),
            pl.BlockSpec((1, D_MODEL), lambda i, j: (0, 0)),
        ],
        out_specs=pl.BlockSpec((bm, D_MODEL), lambda i, j: (i, 0)),
        out_shape=jax.ShapeDtypeStruct((t, D_MODEL), F32),
        scratch_shapes=[pltpu.VMEM((bm, D_MODEL), BF16)],
        compiler_params=pltpu.CompilerParams(
            dimension_semantics=("parallel", "arbitrary"), vmem_limit_bytes=60 * MIB),
        name="swiglu_ffn",
    )(x, g, wgu, wd, fg)


def _rope_tables():
    pos = jnp.arange(SEQ, dtype=jnp.int32)

    def angles(p, dim):
        inv = ROPE_THETA ** (-jnp.arange(0, dim, 2, dtype=F32) / dim)
        return p.astype(F32)[:, None] * inv[None, :]

    ang_b = angles(pos, HEAD_DIM)
    ang_a = angles(pos, A_ROPE)
    ang_r = angles(pos // GRID_W, HEAD_DIM // 2)
    ang_c = angles(pos % GRID_W, HEAD_DIM // 2)
    z = jnp.zeros((SEQ, 32), F32)
    cat = lambda *xs: jnp.concatenate(xs, axis=-1)
    cos_a = cat(jnp.cos(ang_a), z, jnp.cos(ang_a), z)
    sin_a = cat(-jnp.sin(ang_a), z, jnp.sin(ang_a), z)
    cos_b = cat(jnp.cos(ang_b), jnp.cos(ang_b))
    sin_b = cat(-jnp.sin(ang_b), jnp.sin(ang_b))
    cos_c = cat(jnp.cos(ang_r), jnp.cos(ang_c), jnp.cos(ang_r), jnp.cos(ang_c))
    sin_c = cat(-jnp.sin(ang_r), -jnp.sin(ang_c), jnp.sin(ang_r), jnp.sin(ang_c))
    return cos_a, sin_a, cos_b, sin_b, cos_c, sin_c


def _axial_perm():
    a = jnp.arange(32)
    return jnp.concatenate([a, a + 64, a + 32, a + 96])


def _cast_kernel(w_ref, o_ref):
    o_ref[...] = w_ref[...].astype(BF16)


def _cast_layer_bf16(w, layer):
    _, r, c = w.shape
    br = 256
    return pl.pallas_call(
        _cast_kernel,
        grid=(r // br,),
        in_specs=[pl.BlockSpec((None, br, c), lambda i: (layer, i, 0))],
        out_specs=pl.BlockSpec((br, c), lambda i: (i, 0)),
        out_shape=jax.ShapeDtypeStruct((r, c), BF16),
        compiler_params=pltpu.CompilerParams(
            dimension_semantics=("parallel",), vmem_limit_bytes=40 * MIB),
        name="cast_bf16",
    )(w)


def _cast_pair_kernel(a_ref, b_ref, o_ref):
    n = a_ref.shape[1]
    o_ref[:, 0:n] = a_ref[...].astype(BF16)
    o_ref[:, n:2 * n] = b_ref[...].astype(BF16)


def _cast_gate_up(w_gate, w_up, layer):
    br = D_MODEL
    return pl.pallas_call(
        _cast_pair_kernel,
        grid=(D_MODEL // br, D_FF // FFN_BF),
        in_specs=[pl.BlockSpec((None, br, FFN_BF), lambda i, j: (layer, i, j)),
                  pl.BlockSpec((None, br, FFN_BF), lambda i, j: (layer, i, j))],
        out_specs=pl.BlockSpec((br, 2 * FFN_BF), lambda i, j: (i, j)),
        out_shape=jax.ShapeDtypeStruct((D_MODEL, 2 * D_FF), BF16),
        compiler_params=pltpu.CompilerParams(
            dimension_semantics=("parallel", "parallel"), vmem_limit_bytes=40 * MIB),
        name="cast_gate_up",
    )(w_gate, w_up)


def _w_in_layout_kernel(w_ref, o_ref):
    bl = w_ref.shape[1]

    def put(col, rows):
        o_ref[:, col:col + 128] = rows.T.astype(BF16)

    for t in range(OFF_KR // 128):
        put(t * 128, w_ref[t * 128:(t + 1) * 128, :])
    z = jnp.zeros((32, bl), F32)
    put(OFF_KR, jnp.concatenate([w_ref[1024:1056, :], z, w_ref[1056:1088, :], z], axis=0))
    for t in range(3 * B_WIDTH // 128):
        put(OFF_BQ + t * 128, w_ref[1088 + t * 128:1088 + (t + 1) * 128, :])
    for h in range(C_HEADS + C_KV_HEADS):
        r0 = 3392 + h * 128
        put(OFF_CQH + h * 128, jnp.concatenate(
            [w_ref[r0:r0 + 32, :], w_ref[r0 + 64:r0 + 96, :],
             w_ref[r0 + 32:r0 + 64, :], w_ref[r0 + 96:r0 + 128, :]], axis=0))
    for t in range(C_KV_WIDTH // 128):
        put(OFF_CV + t * 128, w_ref[4416 + t * 128:4416 + (t + 1) * 128, :])


def _layout_w_in(w, layer):
    wt = jnp.swapaxes(w, 1, 2)
    bl = 256
    width = wt.shape[1]
    return pl.pallas_call(
        _w_in_layout_kernel,
        grid=(D_MODEL // bl,),
        in_specs=[pl.BlockSpec((None, width, bl), lambda i: (layer, 0, i))],
        out_specs=pl.BlockSpec((bl, PROJ_WIDTH), lambda i: (i, 0)),
        out_shape=jax.ShapeDtypeStruct((D_MODEL, PROJ_WIDTH), BF16),
        compiler_params=pltpu.CompilerParams(
            dimension_semantics=("parallel",), vmem_limit_bytes=40 * MIB),
        name="w_in_layout",
    )(wt)


def _layout_w_uq(w):
    w = w.reshape(A_RANK, A_HEADS, A_NOPE + A_ROPE)
    z = jnp.zeros((A_RANK, A_HEADS, 32), w.dtype)
    out = jnp.concatenate([w[:, :, :128], w[:, :, 128:160], z, w[:, :, 160:192], z], axis=2)
    return out.reshape(A_RANK, A_HEADS * A_DK).astype(BF16)


def _layout_w_ukv(w):
    w = w.reshape(A_RANK, A_HEADS, 2 * HEAD_DIM)
    out = jnp.concatenate([w[:, :, :128].reshape(A_RANK, A_WIDTH),
                           w[:, :, 128:].reshape(A_RANK, A_WIDTH)], axis=1)
    return out.astype(BF16)


def kernel(x, attn_norm, w_in, a_q_norm, a_w_uq, a_kv_norm, a_w_ukv, c_q_norm, c_k_norm,
           out_norm, w_out, ffn_norm, w_gate, w_up, w_down, final_norm):
    bsz, seq, _ = x.shape
    t = bsz * seq
    depth = w_in.shape[0]
    tables = _rope_tables()
    perm = _axial_perm()
    xf = x.reshape(t, D_MODEL)
    for l in range(depth):
        (qa, ka, va, qb1, kb1, vb1, qb4, kb4, vb4, qb16, kb16, vb16, qc, kc, vc) = _proj_prep(
            xf, attn_norm[l][None, :], _layout_w_in(w_in, l), tables,
            a_q_norm[l][None, :], a_kv_norm[l][None, :],
            c_q_norm[l][perm][None, :], c_k_norm[l][perm][None, :],
            _layout_w_uq(a_w_uq[l]), _layout_w_ukv(a_w_ukv[l]))
        r3 = lambda a, n=1: a.reshape(bsz, seq // n, a.shape[1])
        ya = _attention(r3(qa), r3(ka), r3(va), kv_heads=A_HEADS, group=1, dk=A_DK, dv=HEAD_DIM,
                        bq=512 if l == 0 else 1024, ones_row=False, name="mla_attn")
        yb = _dilated([r3(qb1), r3(kb1), r3(vb1), r3(qb4, 4), r3(kb4, 4), r3(vb4, 4),
                       r3(qb16, 16), r3(kb16, 16), r3(vb16, 16)])
        yc = _attention(r3(qc), r3(kc), r3(vc), kv_heads=C_KV_HEADS, group=C_GROUP, dk=HEAD_DIM,
                        dv=HEAD_DIM, bq=256 if l == 0 else 512, ones_row=False, name="gqa_attn")
        xf = _out_proj(ya.reshape(t, A_WIDTH), yb.reshape(t, B_WIDTH), yc.reshape(t, C_WIDTH),
                       xf, out_norm[l][None, :], _cast_layer_bf16(w_out, l))
        xf = _ffn(xf, ffn_norm[l][None, :], _cast_gate_up(w_gate, w_up, l),
                  _cast_layer_bf16(w_down, l), final_norm[None, :], final_norm=(l == depth - 1),
                  bm=1024)
    return xf.reshape(bsz, seq, D_MODEL)
```

```python
import functools

import jax
import jax.numpy as jnp
from jax import lax
from jax.experimental import pallas as pl
from jax.experimental.pallas import tpu as pltpu

F32 = jnp.float32
BF16 = jnp.bfloat16

D_MODEL = 2048
SEQ = 4096
HEAD_DIM = 128
ROPE_THETA = 10000.0
GRID_W = 64
EPS = 1e-6
NEG = -1e30

A_HEADS = 4
A_RANK = 512
A_NOPE = 128
A_ROPE = 64
A_DK = 256
B_HEADS = 6
B_DILATIONS = (1, 4, 16)
B_HALF = 64
C_HEADS = 6
C_KV_HEADS = 2
C_GROUP = C_HEADS // C_KV_HEADS
A_WIDTH = A_HEADS * HEAD_DIM
B_WIDTH = B_HEADS * HEAD_DIM
C_WIDTH = C_HEADS * HEAD_DIM
C_KV_WIDTH = C_KV_HEADS * HEAD_DIM
D_FF = 5632

OFF_CQ = 0
OFF_CKV = 512
OFF_KR = 1024
OFF_BQ = 1152
OFF_BK = OFF_BQ + B_WIDTH
OFF_BV = OFF_BK + B_WIDTH
OFF_CQH = OFF_BV + B_WIDTH
OFF_CK = OFF_CQH + C_WIDTH
OFF_CV = OFF_CK + C_KV_WIDTH
PROJ_USED = OFF_CV + C_KV_WIDTH
PROJ_WIDTH = PROJ_USED

LOG2E = 1.4426950408889634
SCALE_A = (A_NOPE + A_ROPE) ** -0.5 * LOG2E
SCALE_B = HEAD_DIM ** -0.5 * LOG2E
SCALE_C = HEAD_DIM ** -0.5 * LOG2E

MIB = 1024 * 1024


def _rms(x, g):
    ms = jnp.mean(x * x, axis=-1, keepdims=True)
    return x * lax.rsqrt(ms + EPS) * g


def _rope(x, c, s):
    return x * c + pltpu.roll(x, 64, 1) * s


def _resident(shape):
    return pl.BlockSpec(shape, lambda i: (0,) * len(shape), pipeline_mode=pl.Buffered(1))


PREP_BM = 256


def _proj_prep_kernel(x_ref, g_ref, w_ref, cos_a, sin_a, cos_b, sin_b, cos_c, sin_c,
                      gq, gkv, gcq, gck, wuq, wukv, sel_ref,
                      qa, ka, va, qb1, kb1, vb1, qb4, kb4, vb4, qb16, kb16, vb16, qc, kc, vc):
    bm = PREP_BM
    hn = _rms(x_ref[...], g_ref[...]).astype(BF16)

    def project(lo, hi):
        return jnp.dot(hn, w_ref[:, lo:hi], preferred_element_type=F32)

    seg_a = project(0, OFF_BQ)
    seg_bq = project(OFF_BQ, OFF_BK)

    ca, sa = cos_a[...], sin_a[...]
    cq = _rms(seg_a[:, OFF_CQ:OFF_CQ + A_RANK], gq[...]).astype(BF16)
    q = jnp.dot(cq, wuq[...], preferred_element_type=F32)
    for h in range(A_HEADS):
        lo = h * A_DK
        qa[:, lo:lo + 128] = (q[:, lo:lo + 128] * SCALE_A).astype(BF16)
        qa[:, lo + 128:lo + 256] = (_rope(q[:, lo + 128:lo + 256], ca, sa) * SCALE_A).astype(BF16)
    ckv = _rms(seg_a[:, OFF_CKV:OFF_CKV + A_RANK], gkv[...]).astype(BF16)
    kv = jnp.dot(ckv, wukv[...], preferred_element_type=F32)
    kr = _rope(seg_a[:, OFF_KR:OFF_KR + 128], ca, sa).astype(BF16)
    for h in range(A_HEADS):
        lo = h * A_DK
        ka[:, lo:lo + 128] = kv[:, h * 128:(h + 1) * 128].astype(BF16)
        ka[:, lo + 128:lo + 256] = kr
    va[...] = kv[:, A_WIDTH:2 * A_WIDTH].astype(BF16)

    cb, sb = cos_b[...], sin_b[...]
    sel = sel_ref[...]
    ones = jnp.ones((bm, 128), BF16)

    def mixer_b(seg, scale, use_rope, is_v, o1, o4, o16):
        width = 256 if is_v else 128
        tiles = []
        for h in range(B_HEADS):
            x = seg[:, h * 128:(h + 1) * 128]
            if use_rope:
                x = _rope(x, cb, sb)
            if scale is not None:
                x = x * scale
            tiles.append(x.astype(BF16))
            o1[:, h * width:h * width + 128] = tiles[h]
            if is_v:
                o1[:, h * width + 128:(h + 1) * width] = ones
        y = jnp.dot(sel, jnp.concatenate(tiles, axis=1), preferred_element_type=F32)
        for d, od, base in ((4, o4, 0), (16, o16, bm)):
            n = bm // d
            for h in range(B_HEADS):
                for r in range(d):
                    c0 = (h * d + r) * width
                    rows = y[base + r * n:base + (r + 1) * n, h * 128:(h + 1) * 128]
                    od[:, c0:c0 + 128] = rows.astype(BF16)
                    if is_v:
                        od[:, c0 + 128:c0 + 256] = ones[0:n]

    seg_bk = project(OFF_BK, OFF_BV)
    mixer_b(seg_bq, SCALE_B, True, False, qb1, qb4, qb16)
    seg_bv = project(OFF_BV, OFF_CQH)
    mixer_b(seg_bk, None, True, False, kb1, kb4, kb16)
    seg_c = project(OFF_CQH, PROJ_WIDTH)
    mixer_b(seg_bv, None, False, True, vb1, vb4, vb16)

    cc, sc = cos_c[...], sin_c[...]
    for h in range(C_HEADS):
        x = _rms(seg_c[:, h * 128:(h + 1) * 128], gcq[...])
        qc[:, h * 128:(h + 1) * 128] = (_rope(x, cc, sc) * SCALE_C).astype(BF16)
    for h in range(C_KV_HEADS):
        lo = C_WIDTH + h * 128
        x = _rms(seg_c[:, lo:lo + 128], gck[...])
        kc[:, h * 128:(h + 1) * 128] = _rope(x, cc, sc).astype(BF16)
    vc[...] = seg_c[:, C_WIDTH + C_KV_WIDTH:].astype(BF16)


def _proj_prep(x, g, w, tables, gq, gkv, gcq, gck, wuq, wukv):
    t = x.shape[0]
    bm = PREP_BM
    nblk_seq = SEQ // bm
    row = lambda i: (i, 0)
    tab = lambda i: (i % nblk_seq, 0)
    out_widths = [A_HEADS * A_DK, A_HEADS * A_DK, A_WIDTH]
    out_shapes = [jax.ShapeDtypeStruct((t, w_), BF16) for w_ in out_widths]
    out_specs = [pl.BlockSpec((bm, w_), row) for w_ in out_widths]
    for d in B_DILATIONS:
        for w_ in (B_WIDTH, B_WIDTH, 2 * B_WIDTH):
            out_shapes.append(jax.ShapeDtypeStruct((t // d, d * w_), BF16))
            out_specs.append(pl.BlockSpec((bm // d, d * w_), row))
    for w_ in (C_WIDTH, C_KV_WIDTH, C_KV_WIDTH):
        out_shapes.append(jax.ShapeDtypeStruct((t, w_), BF16))
        out_specs.append(pl.BlockSpec((bm, w_), row))
    in_specs = [pl.BlockSpec((bm, D_MODEL), row), _resident((1, D_MODEL)),
                _resident((D_MODEL, PROJ_WIDTH))]
    in_specs += [pl.BlockSpec((bm, 128), tab)] * 6
    in_specs += [_resident((1, A_RANK)), _resident((1, A_RANK)),
                 _resident((1, 128)), _resident((1, 128)),
                 _resident((A_RANK, A_HEADS * A_DK)), _resident((A_RANK, 2 * A_WIDTH)),
                 _resident((2 * bm, bm))]
    return pl.pallas_call(
        _proj_prep_kernel,
        grid=(t // bm,),
        in_specs=in_specs,
        out_specs=out_specs,
        out_shape=out_shapes,
        compiler_params=pltpu.CompilerParams(
            dimension_semantics=("parallel",), vmem_limit_bytes=56 * MIB),
        name="proj_prep",
    )(x, g, w, *tables, gq, gkv, gcq, gck, wuq, wukv, _stream_select(bm))


def _stream_select(bm):
    blocks = []
    for d in B_DILATIONS[1:]:
        out_row = jnp.arange(bm)
        src = (out_row % (bm // d)) * d + out_row // (bm // d)
        blocks.append(jax.nn.one_hot(src, bm, dtype=BF16))
    return jnp.concatenate(blocks, axis=0)


ATTN_KC = 1024
ATTN_SUB = 256


def _attn_kernel(*refs, group, dk, dv, bq, n_side):
    q_ref, k_ref, v_ref = refs[0:3]
    side_in = refs[3:3 + n_side]
    o_ref = refs[3 + n_side]
    side_out = refs[4 + n_side:4 + 2 * n_side]
    vt_scr = refs[4 + 2 * n_side]

    for src, dst in zip(side_in, side_out):
        dst[...] = src[...].astype(BF16)

    @pl.when(pl.program_id(2) == 0)
    def _():
        vt_scr[...] = v_ref[0].T

    q_all = jnp.concatenate([q_ref[0, :, g * dk:(g + 1) * dk] for g in range(group)], axis=0)
    n_chunks = SEQ // ATTN_KC
    n_sub = ATTN_KC // ATTN_SUB

    def scores(c, j):
        r0 = c * ATTN_KC + j * ATTN_SUB
        return lax.dot_general(k_ref[0, r0:r0 + ATTN_SUB, :], q_all, (((1,), (1,)), ((), ())),
                               preferred_element_type=F32)

    def fold8(x, op):
        return op(x.reshape(x.shape[0] // 8, 8, x.shape[1]), axis=0)

    def col_max(pieces):
        mx = fold8(pieces[0], jnp.max)
        for piece in pieces[1:]:
            mx = jnp.maximum(mx, fold8(piece, jnp.max))
        return jnp.max(mx, axis=0, keepdims=True)

    s_cur = [scores(0, j) for j in range(n_sub)]
    m = l = acc = None
    for c in range(n_chunks):
        m_c = col_max(s_cur)
        m_new = m_c if c == 0 else jnp.maximum(m, m_c)
        s_next, p_sum, p_bf = [], None, []
        for j in range(n_sub):
            if c + 1 < n_chunks:
                s_next.append(scores(c + 1, j))
            p = jnp.exp2(s_cur[j] - m_new)
            p_sum = fold8(p, jnp.sum) if j == 0 else p_sum + fold8(p, jnp.sum)
            p_bf.append(p.astype(BF16))
        l_c = jnp.sum(p_sum, axis=0, keepdims=True)
        pv = jnp.dot(vt_scr[:, c * ATTN_KC:(c + 1) * ATTN_KC], jnp.concatenate(p_bf, axis=0),
                     preferred_element_type=F32)
        if c == 0:
            l, acc = l_c, pv
        else:
            alpha = jnp.exp2(m - m_new)
            l = alpha * l + l_c
            acc = alpha * acc + pv
        m = m_new
        s_cur = s_next
    out = acc / l
    for g in range(group):
        o_ref[0, :, g * dv:(g + 1) * dv] = out[:, g * bq:(g + 1) * bq].T


def _attention(q, k, v, *, kv_heads, group, dk, dv, bq, name, side_casts=()):
    b = q.shape[0]
    n_q = SEQ // bq

    def side_block(rows_total, rows):
        last = rows_total // rows - 1
        return lambda bi, h, qi: jnp.minimum((bi * kv_heads + h) * n_q + qi, last)

    in_specs = [
        pl.BlockSpec((1, bq, group * dk), lambda bi, h, qi: (bi, qi, h)),
        pl.BlockSpec((1, SEQ, dk), lambda bi, h, qi: (bi, 0, h)),
        pl.BlockSpec((1, SEQ, dv), lambda bi, h, qi: (bi, 0, h)),
    ]
    out_specs = [pl.BlockSpec((1, bq, group * dv), lambda bi, h, qi: (bi, qi, h))]
    out_shapes = [jax.ShapeDtypeStruct((b, SEQ, kv_heads * group * dv), F32)]
    operands = [q, k, v]
    for w, layer, rows in side_casts:
        _, r, c = w.shape
        assert r % rows == 0 and b * kv_heads * n_q >= r // rows
        blk = side_block(r, rows)
        in_specs.append(pl.BlockSpec(
            (None, rows, c), lambda bi, h, qi, blk=blk, layer=layer: (layer, blk(bi, h, qi), 0)))
        out_specs.append(pl.BlockSpec((rows, c), lambda bi, h, qi, blk=blk: (blk(bi, h, qi), 0)))
        out_shapes.append(jax.ShapeDtypeStruct((r, c), BF16))
        operands.append(w)
    return pl.pallas_call(
        functools.partial(_attn_kernel, group=group, dk=dk, dv=dv, bq=bq, n_side=len(side_casts)),
        grid=(b, kv_heads, n_q),
        in_specs=in_specs,
        out_specs=out_specs,
        out_shape=out_shapes,
        scratch_shapes=[pltpu.VMEM((dv, SEQ), BF16)],
        compiler_params=pltpu.CompilerParams(
            dimension_semantics=("arbitrary", "arbitrary", "arbitrary"),
            vmem_limit_bytes=48 * MIB),
        name=name,
    )(*operands)


DIL_BQ = 128
DIL_WIN = DIL_BQ + 2 * B_HALF


def _dil_block(q_ref, k_ref, v_ref, bias_ref, stream, q0, length):
    if isinstance(q0, int):
        start = min(max(q0 - B_HALF, 0), length - DIL_WIN)
        case = (q0 - start) // B_HALF
    else:
        start = pl.multiple_of(jnp.clip(q0 - B_HALF, 0, length - DIL_WIN), B_HALF)
        case = lax.shift_right_logical(q0 - start, 6)
    q = q_ref[0, pl.ds(q0, DIL_BQ), stream * 128:(stream + 1) * 128]
    k = k_ref[0, pl.ds(start, DIL_WIN), stream * 128:(stream + 1) * 128]
    vx = v_ref[0, pl.ds(start, DIL_WIN), stream * 256:(stream + 1) * 256]
    s = lax.dot_general(q, k, (((1,), (1,)), ((), ())), preferred_element_type=F32)
    s = s + bias_ref[case]
    m = jnp.max(s, axis=-1, keepdims=True)
    p = jnp.exp2(s - m).astype(BF16)
    ol = jnp.dot(p, vx, preferred_element_type=F32)
    l = ol[:, 128:256]
    return ol[:, 0:128] / l, m + jnp.log(l) * LOG2E


def _dil_kernel(q1, k1, v1, q4, k4, v4, q16, k16, v16, bias_ref, o_ref,
                osm4, lsm4, osm16, lsm16, otok, ltok):
    def body1(n, carry):
        q0 = pl.multiple_of(n * DIL_BQ, DIL_BQ)
        o, lse = _dil_block(q1, k1, v1, bias_ref, 0, q0, SEQ)
        otok[0, pl.ds(q0, DIL_BQ), :] = o
        ltok[0, pl.ds(q0, DIL_BQ), :] = lse
        return carry

    lax.fori_loop(0, SEQ // DIL_BQ, body1, 0, unroll=8)

    def body4(n, carry):
        q0 = pl.multiple_of(n * DIL_BQ, DIL_BQ)
        for r in range(4):
            o, lse = _dil_block(q4, k4, v4, bias_ref, r, q0, SEQ // 4)
            osm4[pl.ds(q0, DIL_BQ), r * 128:(r + 1) * 128] = o
            lsm4[pl.ds(q0, DIL_BQ), r * 128:(r + 1) * 128] = lse
        return carry

    lax.fori_loop(0, SEQ // 4 // DIL_BQ, body4, 0, unroll=2)
    for r in range(4):
        otok[1, pl.ds(r, SEQ // 4, stride=4), :] = osm4[:, r * 128:(r + 1) * 128]
        ltok[1, pl.ds(r, SEQ // 4, stride=4), :] = lsm4[:, r * 128:(r + 1) * 128]

    for n in range(SEQ // 16 // DIL_BQ):
        q0 = n * DIL_BQ
        for r in range(16):
            o, lse = _dil_block(q16, k16, v16, bias_ref, r, q0, SEQ // 16)
            osm16[q0:q0 + DIL_BQ, r * 128:(r + 1) * 128] = o
            lsm16[q0:q0 + DIL_BQ, r * 128:(r + 1) * 128] = lse
    for r in range(16):
        otok[2, pl.ds(r, SEQ // 16, stride=16), :] = osm16[:, r * 128:(r + 1) * 128]
        ltok[2, pl.ds(r, SEQ // 16, stride=16), :] = lsm16[:, r * 128:(r + 1) * 128]

    chunk = 256

    def combine(c, carry):
        r0 = pl.multiple_of(c * chunk, chunk)
        l0 = ltok[0, pl.ds(r0, chunk), :]
        l1 = ltok[1, pl.ds(r0, chunk), :]
        l2 = ltok[2, pl.ds(r0, chunk), :]
        mx = jnp.maximum(jnp.maximum(l0, l1), l2)
        e0 = jnp.exp2(l0 - mx)
        e1 = jnp.exp2(l1 - mx)
        e2 = jnp.exp2(l2 - mx)
        num = (e0 * otok[0, pl.ds(r0, chunk), :] + e1 * otok[1, pl.ds(r0, chunk), :]
               + e2 * otok[2, pl.ds(r0, chunk), :])
        o_ref[0, pl.ds(r0, chunk), :] = num / (e0 + e1 + e2)
        return carry

    lax.fori_loop(0, SEQ // chunk, combine, 0)


def _dil_bias():
    i = jnp.arange(DIL_BQ)[:, None]
    j = jnp.arange(DIL_WIN)[None, :]
    return jnp.stack([jnp.where(jnp.abs(j - i - c * B_HALF) <= B_HALF, 0.0, NEG)
                      for c in range(3)]).astype(F32)


def _dilated(qkv_by_dilation):
    b = qkv_by_dilation[0].shape[0]
    in_specs = []
    for d in B_DILATIONS:
        in_specs += [pl.BlockSpec((1, SEQ // d, d * 128), lambda bi, h: (bi, 0, h))] * 2
        in_specs += [pl.BlockSpec((1, SEQ // d, d * 256), lambda bi, h: (bi, 0, h))]
    in_specs += [pl.BlockSpec((3, DIL_BQ, DIL_WIN), lambda bi, h: (0, 0, 0))]
    qkv_by_dilation = list(qkv_by_dilation) + [_dil_bias()]
    return pl.pallas_call(
        _dil_kernel,
        grid=(b, B_HEADS),
        in_specs=in_specs,
        out_specs=pl.BlockSpec((1, SEQ, 128), lambda bi, h: (bi, 0, h)),
        out_shape=jax.ShapeDtypeStruct((b, SEQ, B_WIDTH), F32),
        scratch_shapes=[
            pltpu.VMEM((SEQ // 4, 4 * 128), F32), pltpu.VMEM((SEQ // 4, 4 * 128), F32),
            pltpu.VMEM((SEQ // 16, 16 * 128), F32), pltpu.VMEM((SEQ // 16, 16 * 128), F32),
            pltpu.VMEM((3, SEQ, 128), F32), pltpu.VMEM((3, SEQ, 128), F32),
        ],
        compiler_params=pltpu.CompilerParams(
            dimension_semantics=("parallel", "parallel"), vmem_limit_bytes=56 * MIB),
        name="dilated_attn",
    )(*qkv_by_dilation)


def _out_proj_kernel(ya, yb, yc, x_ref, g_ref, w_ref, o_ref):
    b0, c0 = A_WIDTH, A_WIDTH + B_WIDTH
    na = _rms(ya[...], g_ref[:, 0:b0]).astype(BF16)
    nb = _rms(yb[...], g_ref[:, b0:c0]).astype(BF16)
    nc = _rms(yc[...], g_ref[:, c0:]).astype(BF16)
    y = (jnp.dot(na, w_ref[0:b0, :], preferred_element_type=F32)
         + jnp.dot(nb, w_ref[b0:c0, :], preferred_element_type=F32)
         + jnp.dot(nc, w_ref[c0:, :], preferred_element_type=F32))
    o_ref[...] = x_ref[...] + y


def _out_proj(ya, yb, yc, x, g, w):
    t = x.shape[0]
    bm = 512
    row = lambda i: (i, 0)
    return pl.pallas_call(
        _out_proj_kernel,
        grid=(t // bm,),
        in_specs=[
            pl.BlockSpec((bm, A_WIDTH), row),
            pl.BlockSpec((bm, B_WIDTH), row),
            pl.BlockSpec((bm, C_WIDTH), row),
            pl.BlockSpec((bm, D_MODEL), row),
            _resident((1, D_MODEL)),
            _resident((D_MODEL, D_MODEL)),
        ],
        out_specs=pl.BlockSpec((bm, D_MODEL), row),
        out_shape=jax.ShapeDtypeStruct((t, D_MODEL), F32),
        compiler_params=pltpu.CompilerParams(
            dimension_semantics=("parallel",), vmem_limit_bytes=48 * MIB),
        name="out_proj",
    )(ya, yb, yc, x, g, w)


FFN_BF = 512


def _ffn_kernel(x_ref, g_ref, wg_ref, wu_ref, wd_ref, fg_ref, o_ref, h_scr, *, final_norm):
    j = pl.program_id(1)

    @pl.when(j == 0)
    def _():
        x = x_ref[...]
        h_scr[...] = _rms(x, g_ref[...]).astype(BF16)
        o_ref[...] = x

    h = h_scr[...]
    gate = jnp.dot(h, wg_ref[...], preferred_element_type=F32)
    up = jnp.dot(h, wu_ref[...], preferred_element_type=F32)
    ff = (gate * jax.nn.sigmoid(gate)) * up
    o_ref[...] += jnp.dot(ff.astype(BF16), wd_ref[...], preferred_element_type=F32)

    if final_norm:
        @pl.when(j == pl.num_programs(1) - 1)
        def _():
            o_ref[...] = _rms(o_ref[...], fg_ref[...])


def _ffn(x, g, wg, wu, wd, fg, *, final_norm):
    t = x.shape[0]
    bm, bf = 1024, FFN_BF
    return pl.pallas_call(
        functools.partial(_ffn_kernel, final_norm=final_norm),
        grid=(t // bm, D_FF // bf),
        in_specs=[
            pl.BlockSpec((bm, D_MODEL), lambda i, j: (i, 0)),
            pl.BlockSpec((1, D_MODEL), lambda i, j: (0, 0)),
            pl.BlockSpec((D_MODEL, bf), lambda i, j: (0, j)),
            pl.BlockSpec((D_MODEL, bf), lambda i, j: (0, j)),
            pl.BlockSpec((bf, D_MODEL), lambda i, j: (j, 0)),
            pl.BlockSpec((1, D_MODEL), lambda i, j: (0, 0)),
        ],
        out_specs=pl.BlockSpec((bm, D_MODEL), lambda i, j: (i, 0)),
        out_shape=jax.ShapeDtypeStruct((t, D_MODEL), F32),
        scratch_shapes=[pltpu.VMEM((bm, D_MODEL), BF16)],
        compiler_params=pltpu.CompilerParams(
            dimension_semantics=("parallel", "arbitrary"), vmem_limit_bytes=60 * MIB),
        name="swiglu_ffn",
    )(x, g, wg, wu, wd, fg)


def _rope_tables():
    pos = jnp.arange(SEQ, dtype=jnp.int32)

    def angles(p, dim):
        inv = ROPE_THETA ** (-jnp.arange(0, dim, 2, dtype=F32) / dim)
        return p.astype(F32)[:, None] * inv[None, :]

    ang_b = angles(pos, HEAD_DIM)
    ang_a = angles(pos, A_ROPE)
    ang_r = angles(pos // GRID_W, HEAD_DIM // 2)
    ang_c = angles(pos % GRID_W, HEAD_DIM // 2)
    z = jnp.zeros((SEQ, 32), F32)
    cat = lambda *xs: jnp.concatenate(xs, axis=-1)
    cos_a = cat(jnp.cos(ang_a), z, jnp.cos(ang_a), z)
    sin_a = cat(-jnp.sin(ang_a), z, jnp.sin(ang_a), z)
    cos_b = cat(jnp.cos(ang_b), jnp.cos(ang_b))
    sin_b = cat(-jnp.sin(ang_b), jnp.sin(ang_b))
    cos_c = cat(jnp.cos(ang_r), jnp.cos(ang_c), jnp.cos(ang_r), jnp.cos(ang_c))
    sin_c = cat(-jnp.sin(ang_r), -jnp.sin(ang_c), jnp.sin(ang_r), jnp.sin(ang_c))
    return cos_a, sin_a, cos_b, sin_b, cos_c, sin_c


def _axial_perm():
    a = jnp.arange(32)
    return jnp.concatenate([a, a + 64, a + 32, a + 96])


def _w_in_layout_kernel(w_ref, o_ref):
    bl = w_ref.shape[1]

    def put(col, rows):
        o_ref[:, col:col + 128] = rows.T.astype(BF16)

    for t in range(OFF_KR // 128):
        put(t * 128, w_ref[t * 128:(t + 1) * 128, :])
    z = jnp.zeros((32, bl), F32)
    put(OFF_KR, jnp.concatenate([w_ref[1024:1056, :], z, w_ref[1056:1088, :], z], axis=0))
    for t in range(3 * B_WIDTH // 128):
        put(OFF_BQ + t * 128, w_ref[1088 + t * 128:1088 + (t + 1) * 128, :])
    for h in range(C_HEADS + C_KV_HEADS):
        r0 = 3392 + h * 128
        put(OFF_CQH + h * 128, jnp.concatenate(
            [w_ref[r0:r0 + 32, :], w_ref[r0 + 64:r0 + 96, :],
             w_ref[r0 + 32:r0 + 64, :], w_ref[r0 + 96:r0 + 128, :]], axis=0))
    for t in range(C_KV_WIDTH // 128):
        put(OFF_CV + t * 128, w_ref[4416 + t * 128:4416 + (t + 1) * 128, :])


def _layout_w_in(w, layer):
    wt = jnp.swapaxes(w, 1, 2)
    bl = 256
    width = wt.shape[1]
    return pl.pallas_call(
        _w_in_layout_kernel,
        grid=(D_MODEL // bl,),
        in_specs=[pl.BlockSpec((None, width, bl), lambda i: (layer, 0, i))],
        out_specs=pl.BlockSpec((bl, PROJ_WIDTH), lambda i: (i, 0)),
        out_shape=jax.ShapeDtypeStruct((D_MODEL, PROJ_WIDTH), BF16),
        compiler_params=pltpu.CompilerParams(
            dimension_semantics=("parallel",), vmem_limit_bytes=40 * MIB),
        name="w_in_layout",
    )(wt)


def _layout_w_uq(w):
    w = w.reshape(A_RANK, A_HEADS, A_NOPE + A_ROPE)
    z = jnp.zeros((A_RANK, A_HEADS, 32), w.dtype)
    out = jnp.concatenate([w[:, :, :128], w[:, :, 128:160], z, w[:, :, 160:192], z], axis=2)
    return out.reshape(A_RANK, A_HEADS * A_DK).astype(BF16)


def _layout_w_ukv(w):
    w = w.reshape(A_RANK, A_HEADS, 2 * HEAD_DIM)
    out = jnp.concatenate([w[:, :, :128].reshape(A_RANK, A_WIDTH),
                           w[:, :, 128:].reshape(A_RANK, A_WIDTH)], axis=1)
    return out.astype(BF16)


def kernel(x, attn_norm, w_in, a_q_norm, a_w_uq, a_kv_norm, a_w_ukv, c_q_norm, c_k_norm,
           out_norm, w_out, ffn_norm, w_gate, w_up, w_down, final_norm):
    bsz, seq, _ = x.shape
    t = bsz * seq
    depth = w_in.shape[0]
    tables = _rope_tables()
    perm = _axial_perm()
    xf = x.reshape(t, D_MODEL)
    for l in range(depth):
        (qa, ka, va, qb1, kb1, vb1, qb4, kb4, vb4, qb16, kb16, vb16, qc, kc, vc) = _proj_prep(
            xf, attn_norm[l][None, :], _layout_w_in(w_in, l), tables,
            a_q_norm[l][None, :], a_kv_norm[l][None, :],
            c_q_norm[l][perm][None, :], c_k_norm[l][perm][None, :],
            _layout_w_uq(a_w_uq[l]), _layout_w_ukv(a_w_ukv[l]))
        r3 = lambda a, n=1: a.reshape(bsz, seq // n, a.shape[1])
        ya, wd_bf, wo_bf = _attention(
            r3(qa), r3(ka), r3(va), kv_heads=A_HEADS, group=1, dk=A_DK, dv=HEAD_DIM, bq=512,
            name="mla_attn", side_casts=((w_down, l, 176), (w_out, l, 16)))
        yb = _dilated([r3(qb1), r3(kb1), r3(vb1), r3(qb4, 4), r3(kb4, 4), r3(vb4, 4),
                       r3(qb16, 16), r3(kb16, 16), r3(vb16, 16)])
        yc, wg_bf, wu_bf = _attention(
            r3(qc), r3(kc), r3(vc), kv_heads=C_KV_HEADS, group=C_GROUP, dk=HEAD_DIM, dv=HEAD_DIM,
            bq=256, name="gqa_attn", side_casts=((w_gate, l, 16), (w_up, l, 16)))
        xf = _out_proj(ya.reshape(t, A_WIDTH), yb.reshape(t, B_WIDTH), yc.reshape(t, C_WIDTH),
                       xf, out_norm[l][None, :], wo_bf)
        xf = _ffn(xf, ffn_norm[l][None, :], wg_bf, wu_bf, wd_bf, final_norm[None, :],
                  final_norm=(l == depth - 1))
    return xf.reshape(bsz, seq, D_MODEL)
```

```python
import functools

import jax
import jax.numpy as jnp
from jax import lax
from jax.experimental import pallas as pl
from jax.experimental.pallas import tpu as pltpu

F32 = jnp.float32
BF16 = jnp.bfloat16

D_MODEL = 2048
SEQ = 4096
HEAD_DIM = 128
ROPE_THETA = 10000.0
GRID_W = 64
EPS = 1e-6
NEG = -1e30

A_HEADS = 4
A_RANK = 512
A_NOPE = 128
A_ROPE = 64
A_DK = 256
B_HEADS = 6
B_DILATIONS = (1, 4, 16)
B_HALF = 64
C_HEADS = 6
C_KV_HEADS = 2
C_GROUP = C_HEADS // C_KV_HEADS
A_WIDTH = A_HEADS * HEAD_DIM
B_WIDTH = B_HEADS * HEAD_DIM
C_WIDTH = C_HEADS * HEAD_DIM
C_KV_WIDTH = C_KV_HEADS * HEAD_DIM
D_FF = 5632

OFF_CQ = 0
OFF_CKV = 512
OFF_KR = 1024
OFF_BQ = 1152
OFF_BK = OFF_BQ + B_WIDTH
OFF_BV = OFF_BK + B_WIDTH
OFF_CQH = OFF_BV + B_WIDTH
OFF_CK = OFF_CQH + C_WIDTH
OFF_CV = OFF_CK + C_KV_WIDTH
PROJ_USED = OFF_CV + C_KV_WIDTH
PROJ_WIDTH = PROJ_USED

LOG2E = 1.4426950408889634
SCALE_A = (A_NOPE + A_ROPE) ** -0.5 * LOG2E
SCALE_B = HEAD_DIM ** -0.5 * LOG2E
SCALE_C = HEAD_DIM ** -0.5 * LOG2E

MIB = 1024 * 1024


def _rms(x, g):
    ms = jnp.mean(x * x, axis=-1, keepdims=True)
    return x * lax.rsqrt(ms + EPS) * g


def _rope(x, c, s):
    return x * c + pltpu.roll(x, 64, 1) * s


def _resident(shape):
    return pl.BlockSpec(shape, lambda i: (0,) * len(shape), pipeline_mode=pl.Buffered(1))


PREP_BM = 256


def _proj_prep_kernel(x_ref, g_ref, w_ref, cos_a, sin_a, cos_b, sin_b, cos_c, sin_c,
                      gq, gkv, gcq, gck, wuq, wukv, sel_ref,
                      qa, ka, va, qb1, kb1, vb1, qb4, kb4, vb4, qb16, kb16, vb16, qc, kc, vc):
    bm = PREP_BM
    hn = _rms(x_ref[...], g_ref[...]).astype(BF16)

    def project(lo, hi):
        return jnp.dot(hn, w_ref[:, lo:hi], preferred_element_type=F32)

    seg_a = project(0, OFF_BQ)
    seg_bq = project(OFF_BQ, OFF_BK)

    ca, sa = cos_a[...], sin_a[...]
    cq = _rms(seg_a[:, OFF_CQ:OFF_CQ + A_RANK], gq[...]).astype(BF16)
    q = jnp.dot(cq, wuq[...], preferred_element_type=F32)
    for h in range(A_HEADS):
        lo = h * A_DK
        qa[:, lo:lo + 128] = (q[:, lo:lo + 128] * SCALE_A).astype(BF16)
        qa[:, lo + 128:lo + 256] = (_rope(q[:, lo + 128:lo + 256], ca, sa) * SCALE_A).astype(BF16)
    ckv = _rms(seg_a[:, OFF_CKV:OFF_CKV + A_RANK], gkv[...]).astype(BF16)
    kv = jnp.dot(ckv, wukv[...], preferred_element_type=F32)
    kr = _rope(seg_a[:, OFF_KR:OFF_KR + 128], ca, sa).astype(BF16)
    for h in range(A_HEADS):
        lo = h * A_DK
        ka[:, lo:lo + 128] = kv[:, h * 128:(h + 1) * 128].astype(BF16)
        ka[:, lo + 128:lo + 256] = kr
    va[...] = kv[:, A_WIDTH:2 * A_WIDTH].astype(BF16)

    cb, sb = cos_b[...], sin_b[...]
    sel = sel_ref[...]
    ones = jnp.ones((bm, 128), BF16)

    def mixer_b(seg, scale, use_rope, is_v, o1, o4, o16):
        width = 256 if is_v else 128
        tiles = []
        for h in range(B_HEADS):
            x = seg[:, h * 128:(h + 1) * 128]
            if use_rope:
                x = _rope(x, cb, sb)
            if scale is not None:
                x = x * scale
            tiles.append(x.astype(BF16))
            o1[:, h * width:h * width + 128] = tiles[h]
            if is_v:
                o1[:, h * width + 128:(h + 1) * width] = ones
        y = jnp.dot(sel, jnp.concatenate(tiles, axis=1), preferred_element_type=F32)
        for d, od, base in ((4, o4, 0), (16, o16, bm)):
            n = bm // d
            for h in range(B_HEADS):
                for r in range(d):
                    c0 = (h * d + r) * width
                    rows = y[base + r * n:base + (r + 1) * n, h * 128:(h + 1) * 128]
                    od[:, c0:c0 + 128] = rows.astype(BF16)
                    if is_v:
                        od[:, c0 + 128:c0 + 256] = ones[0:n]

    seg_bk = project(OFF_BK, OFF_BV)
    mixer_b(seg_bq, SCALE_B, True, False, qb1, qb4, qb16)
    seg_bv = project(OFF_BV, OFF_CQH)
    mixer_b(seg_bk, None, True, False, kb1, kb4, kb16)
    seg_c = project(OFF_CQH, PROJ_WIDTH)
    mixer_b(seg_bv, None, False, True, vb1, vb4, vb16)

    cc, sc = cos_c[...], sin_c[...]
    for h in range(C_HEADS):
        x = _rms(seg_c[:, h * 128:(h + 1) * 128], gcq[...])
        qc[:, h * 128:(h + 1) * 128] = (_rope(x, cc, sc) * SCALE_C).astype(BF16)
    for h in range(C_KV_HEADS):
        lo = C_WIDTH + h * 128
        x = _rms(seg_c[:, lo:lo + 128], gck[...])
        kc[:, h * 128:(h + 1) * 128] = _rope(x, cc, sc).astype(BF16)
    vc[...] = seg_c[:, C_WIDTH + C_KV_WIDTH:].astype(BF16)


def _proj_prep(x, g, w, tables, gq, gkv, gcq, gck, wuq, wukv):
    t = x.shape[0]
    bm = PREP_BM
    nblk_seq = SEQ // bm
    row = lambda i: (i, 0)
    tab = lambda i: (i % nblk_seq, 0)
    out_widths = [A_HEADS * A_DK, A_HEADS * A_DK, A_WIDTH]
    out_shapes = [jax.ShapeDtypeStruct((t, w_), BF16) for w_ in out_widths]
    out_specs = [pl.BlockSpec((bm, w_), row) for w_ in out_widths]
    for d in B_DILATIONS:
        for w_ in (B_WIDTH, B_WIDTH, 2 * B_WIDTH):
            out_shapes.append(jax.ShapeDtypeStruct((t // d, d * w_), BF16))
            out_specs.append(pl.BlockSpec((bm // d, d * w_), row))
    for w_ in (C_WIDTH, C_KV_WIDTH, C_KV_WIDTH):
        out_shapes.append(jax.ShapeDtypeStruct((t, w_), BF16))
        out_specs.append(pl.BlockSpec((bm, w_), row))
    in_specs = [pl.BlockSpec((bm, D_MODEL), row), _resident((1, D_MODEL)),
                _resident((D_MODEL, PROJ_WIDTH))]
    in_specs += [pl.BlockSpec((bm, 128), tab)] * 6
    in_specs += [_resident((1, A_RANK)), _resident((1, A_RANK)),
                 _resident((1, 128)), _resident((1, 128)),
                 _resident((A_RANK, A_HEADS * A_DK)), _resident((A_RANK, 2 * A_WIDTH)),
                 _resident((2 * bm, bm))]
    return pl.pallas_call(
        _proj_prep_kernel,
        grid=(t // bm,),
        in_specs=in_specs,
        out_specs=out_specs,
        out_shape=out_shapes,
        compiler_params=pltpu.CompilerParams(
            dimension_semantics=("parallel",), vmem_limit_bytes=56 * MIB),
        name="proj_prep",
    )(x, g, w, *tables, gq, gkv, gcq, gck, wuq, wukv, _stream_select(bm))


def _stream_select(bm):
    blocks = []
    for d in B_DILATIONS[1:]:
        out_row = jnp.arange(bm)
        src = (out_row % (bm // d)) * d + out_row // (bm // d)
        blocks.append(jax.nn.one_hot(src, bm, dtype=BF16))
    return jnp.concatenate(blocks, axis=0)


ATTN_KC = 1024
ATTN_SUB = 256


def _attn_kernel(*refs, group, dk, dv, bq, n_side, sub):
    q_ref, k_ref, v_ref = refs[0:3]
    side_in = refs[3:3 + n_side]
    o_ref = refs[3 + n_side]
    side_out = refs[4 + n_side:4 + 2 * n_side]
    vt_scr = refs[4 + 2 * n_side]

    for src, dst in zip(side_in, side_out):
        dst[...] = src[...].astype(BF16)

    @pl.when(pl.program_id(2) == 0)
    def _():
        vt_scr[...] = v_ref[0].T

    q_all = jnp.concatenate([q_ref[0, :, g * dk:(g + 1) * dk] for g in range(group)], axis=0)
    n_chunks = SEQ // ATTN_KC
    n_sub = ATTN_KC // sub

    def scores(c, j):
        r0 = c * ATTN_KC + j * sub
        return lax.dot_general(k_ref[0, r0:r0 + sub, :], q_all, (((1,), (1,)), ((), ())),
                               preferred_element_type=F32)

    def fold8(x, op):
        return op(x.reshape(x.shape[0] // 8, 8, x.shape[1]), axis=0)

    def col_max(pieces):
        mx = fold8(pieces[0], jnp.max)
        for piece in pieces[1:]:
            mx = jnp.maximum(mx, fold8(piece, jnp.max))
        return jnp.max(mx, axis=0, keepdims=True)

    s_cur = [scores(0, j) for j in range(n_sub)]
    m = l = acc = None
    for c in range(n_chunks):
        m_c = col_max(s_cur)
        m_new = m_c if c == 0 else jnp.maximum(m, m_c)
        s_next, p_sum, p_bf = [], None, []
        for j in range(n_sub):
            if c + 1 < n_chunks:
                s_next.append(scores(c + 1, j))
            p = jnp.exp2(s_cur[j] - m_new)
            p_sum = fold8(p, jnp.sum) if j == 0 else p_sum + fold8(p, jnp.sum)
            p_bf.append(p.astype(BF16))
        l_c = jnp.sum(p_sum, axis=0, keepdims=True)
        pv = jnp.dot(vt_scr[:, c * ATTN_KC:(c + 1) * ATTN_KC], jnp.concatenate(p_bf, axis=0),
                     preferred_element_type=F32)
        if c == 0:
            l, acc = l_c, pv
        else:
            alpha = jnp.exp2(m - m_new)
            l = alpha * l + l_c
            acc = alpha * acc + pv
        m = m_new
        s_cur = s_next
    out = acc / l
    for g in range(group):
        o_ref[0, :, g * dv:(g + 1) * dv] = out[:, g * bq:(g + 1) * bq].T


def _attention(q, k, v, *, kv_heads, group, dk, dv, bq, name, side_casts=(), sub=ATTN_SUB):
    b = q.shape[0]
    n_q = SEQ // bq

    def side_block(rows_total, rows):
        last = rows_total // rows - 1
        return lambda bi, h, qi: jnp.minimum((bi * kv_heads + h) * n_q + qi, last)

    in_specs = [
        pl.BlockSpec((1, bq, group * dk), lambda bi, h, qi: (bi, qi, h)),
        pl.BlockSpec((1, SEQ, dk), lambda bi, h, qi: (bi, 0, h)),
        pl.BlockSpec((1, SEQ, dv), lambda bi, h, qi: (bi, 0, h)),
    ]
    out_specs = [pl.BlockSpec((1, bq, group * dv), lambda bi, h, qi: (bi, qi, h))]
    out_shapes = [jax.ShapeDtypeStruct((b, SEQ, kv_heads * group * dv), F32)]
    operands = [q, k, v]
    for w, layer, rows in side_casts:
        _, r, c = w.shape
        assert r % rows == 0 and b * kv_heads * n_q >= r // rows
        blk = side_block(r, rows)
        in_specs.append(pl.BlockSpec(
            (None, rows, c), lambda bi, h, qi, blk=blk, layer=layer: (layer, blk(bi, h, qi), 0)))
        out_specs.append(pl.BlockSpec((rows, c), lambda bi, h, qi, blk=blk: (blk(bi, h, qi), 0)))
        out_shapes.append(jax.ShapeDtypeStruct((r, c), BF16))
        operands.append(w)
    return pl.pallas_call(
        functools.partial(_attn_kernel, group=group, dk=dk, dv=dv, bq=bq, n_side=len(side_casts),
                          sub=sub),
        grid=(b, kv_heads, n_q),
        in_specs=in_specs,
        out_specs=out_specs,
        out_shape=out_shapes,
        scratch_shapes=[pltpu.VMEM((dv, SEQ), BF16)],
        compiler_params=pltpu.CompilerParams(
            dimension_semantics=("arbitrary", "arbitrary", "arbitrary"),
            vmem_limit_bytes=48 * MIB),
        name=name,
    )(*operands)


DIL_BQ = 128
DIL_WIN = DIL_BQ + 2 * B_HALF


def _dil_block(q_ref, k_ref, v_ref, bias_ref, stream, q0, length):
    if isinstance(q0, int):
        start = min(max(q0 - B_HALF, 0), length - DIL_WIN)
        case = (q0 - start) // B_HALF
    else:
        start = pl.multiple_of(jnp.clip(q0 - B_HALF, 0, length - DIL_WIN), B_HALF)
        case = lax.shift_right_logical(q0 - start, 6)
    q = q_ref[0, pl.ds(q0, DIL_BQ), stream * 128:(stream + 1) * 128]
    k = k_ref[0, pl.ds(start, DIL_WIN), stream * 128:(stream + 1) * 128]
    vx = v_ref[0, pl.ds(start, DIL_WIN), stream * 256:(stream + 1) * 256]
    s = lax.dot_general(q, k, (((1,), (1,)), ((), ())), preferred_element_type=F32)
    s = s + bias_ref[case]
    m = jnp.max(s, axis=-1, keepdims=True)
    p = jnp.exp2(s - m).astype(BF16)
    ol = jnp.dot(p, vx, preferred_element_type=F32)
    l = ol[:, 128:256]
    return ol[:, 0:128] / l, m + jnp.log(l) * LOG2E


def _dil_kernel(q1, k1, v1, q4, k4, v4, q16, k16, v16, bias_ref, o_ref,
                osm4, lsm4, osm16, lsm16, otok, ltok):
    def body1(n, carry):
        q0 = pl.multiple_of(n * DIL_BQ, DIL_BQ)
        o, lse = _dil_block(q1, k1, v1, bias_ref, 0, q0, SEQ)
        otok[0, pl.ds(q0, DIL_BQ), :] = o
        ltok[0, pl.ds(q0, DIL_BQ), :] = lse
        return carry

    lax.fori_loop(0, SEQ // DIL_BQ, body1, 0, unroll=16)

    def body4(n, carry):
        q0 = pl.multiple_of(n * DIL_BQ, DIL_BQ)
        for r in range(4):
            o, lse = _dil_block(q4, k4, v4, bias_ref, r, q0, SEQ // 4)
            osm4[pl.ds(q0, DIL_BQ), r * 128:(r + 1) * 128] = o
            lsm4[pl.ds(q0, DIL_BQ), r * 128:(r + 1) * 128] = lse
        return carry

    lax.fori_loop(0, SEQ // 4 // DIL_BQ, body4, 0, unroll=4)
    for r in range(4):
        otok[1, pl.ds(r, SEQ // 4, stride=4), :] = osm4[:, r * 128:(r + 1) * 128]
        ltok[1, pl.ds(r, SEQ // 4, stride=4), :] = lsm4[:, r * 128:(r + 1) * 128]

    for n in range(SEQ // 16 // DIL_BQ):
        q0 = n * DIL_BQ
        for r in range(16):
            o, lse = _dil_block(q16, k16, v16, bias_ref, r, q0, SEQ // 16)
            osm16[q0:q0 + DIL_BQ, r * 128:(r + 1) * 128] = o
            lsm16[q0:q0 + DIL_BQ, r * 128:(r + 1) * 128] = lse
    for r in range(16):
        otok[2, pl.ds(r, SEQ // 16, stride=16), :] = osm16[:, r * 128:(r + 1) * 128]
        ltok[2, pl.ds(r, SEQ // 16, stride=16), :] = lsm16[:, r * 128:(r + 1) * 128]

    chunk = 256

    def combine(c, carry):
        r0 = pl.multiple_of(c * chunk, chunk)
        l0 = ltok[0, pl.ds(r0, chunk), :]
        l1 = ltok[1, pl.ds(r0, chunk), :]
        l2 = ltok[2, pl.ds(r0, chunk), :]
        mx = jnp.maximum(jnp.maximum(l0, l1), l2)
        e0 = jnp.exp2(l0 - mx)
        e1 = jnp.exp2(l1 - mx)
        e2 = jnp.exp2(l2 - mx)
        num = (e0 * otok[0, pl.ds(r0, chunk), :] + e1 * otok[1, pl.ds(r0, chunk), :]
               + e2 * otok[2, pl.ds(r0, chunk), :])
        o_ref[0, pl.ds(r0, chunk), :] = num / (e0 + e1 + e2)
        return carry

    lax.fori_loop(0, SEQ // chunk, combine, 0)


def _dil_bias():
    i = jnp.arange(DIL_BQ)[:, None]
    j = jnp.arange(DIL_WIN)[None, :]
    return jnp.stack([jnp.where(jnp.abs(j - i - c * B_HALF) <= B_HALF, 0.0, NEG)
                      for c in range(3)]).astype(F32)


def _dilated(qkv_by_dilation):
    b = qkv_by_dilation[0].shape[0]
    in_specs = []
    for d in B_DILATIONS:
        in_specs += [pl.BlockSpec((1, SEQ // d, d * 128), lambda bi, h: (bi, 0, h))] * 2
        in_specs += [pl.BlockSpec((1, SEQ // d, d * 256), lambda bi, h: (bi, 0, h))]
    in_specs += [pl.BlockSpec((3, DIL_BQ, DIL_WIN), lambda bi, h: (0, 0, 0))]
    qkv_by_dilation = list(qkv_by_dilation) + [_dil_bias()]
    return pl.pallas_call(
        _dil_kernel,
        grid=(b, B_HEADS),
        in_specs=in_specs,
        out_specs=pl.BlockSpec((1, SEQ, 128), lambda bi, h: (bi, 0, h)),
        out_shape=jax.ShapeDtypeStruct((b, SEQ, B_WIDTH), F32),
        scratch_shapes=[
            pltpu.VMEM((SEQ // 4, 4 * 128), F32), pltpu.VMEM((SEQ // 4, 4 * 128), F32),
            pltpu.VMEM((SEQ // 16, 16 * 128), F32), pltpu.VMEM((SEQ // 16, 16 * 128), F32),
            pltpu.VMEM((3, SEQ, 128), F32), pltpu.VMEM((3, SEQ, 128), F32),
        ],
        compiler_params=pltpu.CompilerParams(
            dimension_semantics=("parallel", "parallel"), vmem_limit_bytes=56 * MIB),
        name="dilated_attn",
    )(*qkv_by_dilation)


def _out_proj_kernel(ya, yb, yc, x_ref, g_ref, w_ref, o_ref):
    b0, c0 = A_WIDTH, A_WIDTH + B_WIDTH
    na = _rms(ya[...], g_ref[:, 0:b0]).astype(BF16)
    nb = _rms(yb[...], g_ref[:, b0:c0]).astype(BF16)
    nc = _rms(yc[...], g_ref[:, c0:]).astype(BF16)
    y = (jnp.dot(na, w_ref[0:b0, :], preferred_element_type=F32)
         + jnp.dot(nb, w_ref[b0:c0, :], preferred_element_type=F32)
         + jnp.dot(nc, w_ref[c0:, :], preferred_element_type=F32))
    o_ref[...] = x_ref[...] + y


def _out_proj(ya, yb, yc, x, g, w):
    t = x.shape[0]
    bm = 512
    row = lambda i: (i, 0)
    return pl.pallas_call(
        _out_proj_kernel,
        grid=(t // bm,),
        in_specs=[
            pl.BlockSpec((bm, A_WIDTH), row),
            pl.BlockSpec((bm, B_WIDTH), row),
            pl.BlockSpec((bm, C_WIDTH), row),
            pl.BlockSpec((bm, D_MODEL), row),
            _resident((1, D_MODEL)),
            _resident((D_MODEL, D_MODEL)),
        ],
        out_specs=pl.BlockSpec((bm, D_MODEL), row),
        out_shape=jax.ShapeDtypeStruct((t, D_MODEL), F32),
        compiler_params=pltpu.CompilerParams(
            dimension_semantics=("parallel",), vmem_limit_bytes=48 * MIB),
        name="out_proj",
    )(ya, yb, yc, x, g, w)


FFN_BF = 512


def _ffn_kernel(x_ref, g_ref, wg_ref, wu_ref, wd_ref, fg_ref, o_ref, h_scr, *, final_norm):
    j = pl.program_id(1)

    @pl.when(j == 0)
    def _():
        x = x_ref[...]
        h_scr[...] = _rms(x, g_ref[...]).astype(BF16)
        o_ref[...] = x

    h = h_scr[...]
    gate = jnp.dot(h, wg_ref[...], preferred_element_type=F32)
    up = jnp.dot(h, wu_ref[...], preferred_element_type=F32)
    ff = (gate * jax.nn.sigmoid(gate)) * up
    o_ref[...] += jnp.dot(ff.astype(BF16), wd_ref[...], preferred_element_type=F32)

    if final_norm:
        @pl.when(j == pl.num_programs(1) - 1)
        def _():
            o_ref[...] = _rms(o_ref[...], fg_ref[...])


def _ffn(x, g, wg, wu, wd, fg, *, final_norm):
    t = x.shape[0]
    bm, bf = 1024, FFN_BF
    return pl.pallas_call(
        functools.partial(_ffn_kernel, final_norm=final_norm),
        grid=(t // bm, D_FF // bf),
        in_specs=[
            pl.BlockSpec((bm, D_MODEL), lambda i, j: (i, 0)),
            pl.BlockSpec((1, D_MODEL), lambda i, j: (0, 0)),
            pl.BlockSpec((D_MODEL, bf), lambda i, j: (0, j)),
            pl.BlockSpec((D_MODEL, bf), lambda i, j: (0, j)),
            pl.BlockSpec((bf, D_MODEL), lambda i, j: (j, 0)),
            pl.BlockSpec((1, D_MODEL), lambda i, j: (0, 0)),
        ],
        out_specs=pl.BlockSpec((bm, D_MODEL), lambda i, j: (i, 0)),
        out_shape=jax.ShapeDtypeStruct((t, D_MODEL), F32),
        scratch_shapes=[pltpu.VMEM((bm, D_MODEL), BF16)],
        compiler_params=pltpu.CompilerParams(
            dimension_semantics=("parallel", "arbitrary"), vmem_limit_bytes=60 * MIB),
        name="swiglu_ffn",
    )(x, g, wg, wu, wd, fg)


def _rope_tables():
    pos = jnp.arange(SEQ, dtype=jnp.int32)

    def angles(p, dim):
        inv = ROPE_THETA ** (-jnp.arange(0, dim, 2, dtype=F32) / dim)
        return p.astype(F32)[:, None] * inv[None, :]

    ang_b = angles(pos, HEAD_DIM)
    ang_a = angles(pos, A_ROPE)
    ang_g = angles(jnp.arange(GRID_W, dtype=jnp.int32), HEAD_DIM // 2)
    cos_g, sin_g = jnp.cos(ang_g), jnp.sin(ang_g)
    cos_r, sin_r = (jnp.repeat(a, GRID_W, axis=0) for a in (cos_g, sin_g))
    cos_k, sin_k = (jnp.tile(a, (SEQ // GRID_W, 1)) for a in (cos_g, sin_g))
    cos_pa, sin_pa = jnp.cos(ang_a), jnp.sin(ang_a)
    cos_pb, sin_pb = jnp.cos(ang_b), jnp.sin(ang_b)
    z = jnp.zeros((SEQ, 32), F32)
    cat = lambda *xs: jnp.concatenate(xs, axis=-1)
    cos_a = cat(cos_pa, z, cos_pa, z)
    sin_a = cat(-sin_pa, z, sin_pa, z)
    cos_b = cat(cos_pb, cos_pb)
    sin_b = cat(-sin_pb, sin_pb)
    cos_c = cat(cos_r, cos_k, cos_r, cos_k)
    sin_c = cat(-sin_r, -sin_k, sin_r, sin_k)
    return cos_a, sin_a, cos_b, sin_b, cos_c, sin_c


def _axial_perm():
    a = jnp.arange(32)
    return jnp.concatenate([a, a + 64, a + 32, a + 96])


def _w_in_layout_kernel(w_ref, o_ref):
    bl = w_ref.shape[1]

    def put(col, rows):
        o_ref[:, col:col + 128] = rows.T.astype(BF16)

    for t in range(OFF_KR // 128):
        put(t * 128, w_ref[t * 128:(t + 1) * 128, :])
    z = jnp.zeros((32, bl), F32)
    put(OFF_KR, jnp.concatenate([w_ref[1024:1056, :], z, w_ref[1056:1088, :], z], axis=0))
    for t in range(3 * B_WIDTH // 128):
        put(OFF_BQ + t * 128, w_ref[1088 + t * 128:1088 + (t + 1) * 128, :])
    for h in range(C_HEADS + C_KV_HEADS):
        r0 = 3392 + h * 128
        put(OFF_CQH + h * 128, jnp.concatenate(
            [w_ref[r0:r0 + 32, :], w_ref[r0 + 64:r0 + 96, :],
             w_ref[r0 + 32:r0 + 64, :], w_ref[r0 + 96:r0 + 128, :]], axis=0))
    for t in range(C_KV_WIDTH // 128):
        put(OFF_CV + t * 128, w_ref[4416 + t * 128:4416 + (t + 1) * 128, :])


def _layout_w_in(w, layer):
    wt = jnp.swapaxes(w, 1, 2)
    bl = 256
    width = wt.shape[1]
    return pl.pallas_call(
        _w_in_layout_kernel,
        grid=(D_MODEL // bl,),
        in_specs=[pl.BlockSpec((None, width, bl), lambda i: (layer, 0, i))],
        out_specs=pl.BlockSpec((bl, PROJ_WIDTH), lambda i: (i, 0)),
        out_shape=jax.ShapeDtypeStruct((D_MODEL, PROJ_WIDTH), BF16),
        compiler_params=pltpu.CompilerParams(
            dimension_semantics=("parallel",), vmem_limit_bytes=40 * MIB),
        name="w_in_layout",
    )(wt)


def _layout_w_uq(w):
    w = w.reshape(A_RANK, A_HEADS, A_NOPE + A_ROPE)
    z = jnp.zeros((A_RANK, A_HEADS, 32), w.dtype)
    out = jnp.concatenate([w[:, :, :128], w[:, :, 128:160], z, w[:, :, 160:192], z], axis=2)
    return out.reshape(A_RANK, A_HEADS * A_DK).astype(BF16)


def _layout_w_ukv(w):
    w = w.reshape(A_RANK, A_HEADS, 2 * HEAD_DIM)
    out = jnp.concatenate([w[:, :, :128].reshape(A_RANK, A_WIDTH),
                           w[:, :, 128:].reshape(A_RANK, A_WIDTH)], axis=1)
    return out.astype(BF16)


def kernel(x, attn_norm, w_in, a_q_norm, a_w_uq, a_kv_norm, a_w_ukv, c_q_norm, c_k_norm,
           out_norm, w_out, ffn_norm, w_gate, w_up, w_down, final_norm):
    bsz, seq, _ = x.shape
    t = bsz * seq
    depth = w_in.shape[0]
    tables = _rope_tables()
    perm = _axial_perm()
    xf = x.reshape(t, D_MODEL)
    for l in range(depth):
        (qa, ka, va, qb1, kb1, vb1, qb4, kb4, vb4, qb16, kb16, vb16, qc, kc, vc) = _proj_prep(
            xf, attn_norm[l][None, :], _layout_w_in(w_in, l), tables,
            a_q_norm[l][None, :], a_kv_norm[l][None, :],
            c_q_norm[l][perm][None, :], c_k_norm[l][perm][None, :],
            _layout_w_uq(a_w_uq[l]), _layout_w_ukv(a_w_ukv[l]))
        r3 = lambda a, n=1: a.reshape(bsz, seq // n, a.shape[1])
        ya, wd_bf, wo_bf = _attention(
            r3(qa), r3(ka), r3(va), kv_heads=A_HEADS, group=1, dk=A_DK, dv=HEAD_DIM, bq=512,
            name="mla_attn", side_casts=((w_down, l, 176), (w_out, l, 16)),
            sub=256 if l == 0 else 1024)
        yb = _dilated([r3(qb1), r3(kb1), r3(vb1), r3(qb4, 4), r3(kb4, 4), r3(vb4, 4),
                       r3(qb16, 16), r3(kb16, 16), r3(vb16, 16)])
        yc, wg_bf, wu_bf = _attention(
            r3(qc), r3(kc), r3(vc), kv_heads=C_KV_HEADS, group=C_GROUP, dk=HEAD_DIM, dv=HEAD_DIM,
            bq=256, name="gqa_attn", side_casts=((w_gate, l, 16), (w_up, l, 16)),
            sub=256 if l == 0 else 1024)
        xf = _out_proj(ya.reshape(t, A_WIDTH), yb.reshape(t, B_WIDTH), yc.reshape(t, C_WIDTH),
                       xf, out_norm[l][None, :], wo_bf)
        xf = _ffn(xf, ffn_norm[l][None, :], wg_bf, wu_bf, wd_bf, final_norm[None, :],
                  final_norm=(l == depth - 1))
    return xf.reshape(bsz, seq, D_MODEL)
```

```python
import functools

import jax
import jax.numpy as jnp
from jax import lax
from jax.experimental import pallas as pl
from jax.experimental.pallas import tpu as pltpu

F32 = jnp.float32
BF16 = jnp.bfloat16

D_MODEL = 2048
SEQ = 4096
HEAD_DIM = 128
ROPE_THETA = 10000.0
GRID_W = 64
EPS = 1e-6
NEG = -1e30

A_HEADS = 4
A_RANK = 512
A_NOPE = 128
A_ROPE = 64
A_DK = 256
B_HEADS = 6
B_DILATIONS = (1, 4, 16)
B_HALF = 64
C_HEADS = 6
C_KV_HEADS = 2
C_GROUP = C_HEADS // C_KV_HEADS
A_WIDTH = A_HEADS * HEAD_DIM
B_WIDTH = B_HEADS * HEAD_DIM
C_WIDTH = C_HEADS * HEAD_DIM
C_KV_WIDTH = C_KV_HEADS * HEAD_DIM
D_FF = 5632

OFF_CQ = 0
OFF_CKV = 512
OFF_KR = 1024
OFF_BQ = 1152
OFF_BK = OFF_BQ + B_WIDTH
OFF_BV = OFF_BK + B_WIDTH
OFF_CQH = OFF_BV + B_WIDTH
OFF_CK = OFF_CQH + C_WIDTH
OFF_CV = OFF_CK + C_KV_WIDTH
PROJ_USED = OFF_CV + C_KV_WIDTH
PROJ_WIDTH = PROJ_USED

LOG2E = 1.4426950408889634
SCALE_A = (A_NOPE + A_ROPE) ** -0.5 * LOG2E
SCALE_B = HEAD_DIM ** -0.5 * LOG2E
SCALE_C = HEAD_DIM ** -0.5 * LOG2E

MIB = 1024 * 1024


def _rms(x, g):
    ms = jnp.mean(x * x, axis=-1, keepdims=True)
    return x * lax.rsqrt(ms + EPS) * g


def _rope(x, c, s):
    return x * c + pltpu.roll(x, 64, 1) * s


def _resident(shape):
    return pl.BlockSpec(shape, lambda i: (0,) * len(shape), pipeline_mode=pl.Buffered(1))


PREP_BM = 256


def _proj_prep_kernel(x_ref, g_ref, w_ref, cos_a, sin_a, cos_b, sin_b, cos_c, sin_c,
                      gq, gkv, gcq, gck, wuq, wukv, sel_ref,
                      qa, ka, va, qb1, kb1, vb1, qb4, kb4, vb4, qb16, kb16, vb16, qc, kc, vc):
    bm = PREP_BM
    hn = _rms(x_ref[...], g_ref[...]).astype(BF16)

    def project(lo, hi):
        return jnp.dot(hn, w_ref[:, lo:hi], preferred_element_type=F32)

    seg_a = project(0, OFF_BQ)
    seg_bq = project(OFF_BQ, OFF_BK)

    ca, sa = cos_a[...], sin_a[...]
    cq = _rms(seg_a[:, OFF_CQ:OFF_CQ + A_RANK], gq[...]).astype(BF16)
    q = jnp.dot(cq, wuq[...], preferred_element_type=F32)
    for h in range(A_HEADS):
        lo = h * A_DK
        qa[:, lo:lo + 128] = (q[:, lo:lo + 128] * SCALE_A).astype(BF16)
        qa[:, lo + 128:lo + 256] = (_rope(q[:, lo + 128:lo + 256], ca, sa) * SCALE_A).astype(BF16)
    ckv = _rms(seg_a[:, OFF_CKV:OFF_CKV + A_RANK], gkv[...]).astype(BF16)
    kv = jnp.dot(ckv, wukv[...], preferred_element_type=F32)
    kr = _rope(seg_a[:, OFF_KR:OFF_KR + 128], ca, sa).astype(BF16)
    for h in range(A_HEADS):
        lo = h * A_DK
        ka[:, lo:lo + 128] = kv[:, h * 128:(h + 1) * 128].astype(BF16)
        ka[:, lo + 128:lo + 256] = kr
    va[...] = kv[:, A_WIDTH:2 * A_WIDTH].astype(BF16)

    cb, sb = cos_b[...], sin_b[...]
    sel = sel_ref[...]
    ones = jnp.ones((bm, 128), BF16)

    def mixer_b(seg, scale, use_rope, is_v, o1, o4, o16):
        width = 256 if is_v else 128
        tiles = []
        for h in range(B_HEADS):
            x = seg[:, h * 128:(h + 1) * 128]
            if use_rope:
                x = _rope(x, cb, sb)
            if scale is not None:
                x = x * scale
            tiles.append(x.astype(BF16))
            o1[:, h * width:h * width + 128] = tiles[h]
            if is_v:
                o1[:, h * width + 128:(h + 1) * width] = ones
        y = jnp.dot(sel, jnp.concatenate(tiles, axis=1), preferred_element_type=F32)
        for d, od, base in ((4, o4, 0), (16, o16, bm)):
            n = bm // d
            for h in range(B_HEADS):
                for r in range(d):
                    c0 = (h * d + r) * width
                    rows = y[base + r * n:base + (r + 1) * n, h * 128:(h + 1) * 128]
                    od[:, c0:c0 + 128] = rows.astype(BF16)
                    if is_v:
                        od[:, c0 + 128:c0 + 256] = ones[0:n]

    seg_bk = project(OFF_BK, OFF_BV)
    mixer_b(seg_bq, SCALE_B, True, False, qb1, qb4, qb16)
    seg_bv = project(OFF_BV, OFF_CQH)
    mixer_b(seg_bk, None, True, False, kb1, kb4, kb16)
    seg_c = project(OFF_CQH, PROJ_WIDTH)
    mixer_b(seg_bv, None, False, True, vb1, vb4, vb16)

    cc, sc = cos_c[...], sin_c[...]
    for h in range(C_HEADS):
        x = _rms(seg_c[:, h * 128:(h + 1) * 128], gcq[...])
        qc[:, h * 128:(h + 1) * 128] = (_rope(x, cc, sc) * SCALE_C).astype(BF16)
    for h in range(C_KV_HEADS):
        lo = C_WIDTH + h * 128
        x = _rms(seg_c[:, lo:lo + 128], gck[...])
        kc[:, h * 128:(h + 1) * 128] = _rope(x, cc, sc).astype(BF16)
    vc[...] = seg_c[:, C_WIDTH + C_KV_WIDTH:].astype(BF16)


def _proj_prep(x, g, w, tables, gq, gkv, gcq, gck, wuq, wukv):
    t = x.shape[0]
    bm = PREP_BM
    nblk_seq = SEQ // bm
    row = lambda i: (i, 0)
    tab = lambda i: (i % nblk_seq, 0)
    out_widths = [A_HEADS * A_DK, A_HEADS * A_DK, A_WIDTH]
    out_shapes = [jax.ShapeDtypeStruct((t, w_), BF16) for w_ in out_widths]
    out_specs = [pl.BlockSpec((bm, w_), row) for w_ in out_widths]
    for d in B_DILATIONS:
        for w_ in (B_WIDTH, B_WIDTH, 2 * B_WIDTH):
            out_shapes.append(jax.ShapeDtypeStruct((t // d, d * w_), BF16))
            out_specs.append(pl.BlockSpec((bm // d, d * w_), row))
    for w_ in (C_WIDTH, C_KV_WIDTH, C_KV_WIDTH):
        out_shapes.append(jax.ShapeDtypeStruct((t, w_), BF16))
        out_specs.append(pl.BlockSpec((bm, w_), row))
    in_specs = [pl.BlockSpec((bm, D_MODEL), row), _resident((1, D_MODEL)),
                _resident((D_MODEL, PROJ_WIDTH))]
    in_specs += [pl.BlockSpec((bm, 128), tab)] * 6
    in_specs += [_resident((1, A_RANK)), _resident((1, A_RANK)),
                 _resident((1, 128)), _resident((1, 128)),
                 _resident((A_RANK, A_HEADS * A_DK)), _resident((A_RANK, 2 * A_WIDTH)),
                 _resident((2 * bm, bm))]
    return pl.pallas_call(
        _proj_prep_kernel,
        grid=(t // bm,),
        in_specs=in_specs,
        out_specs=out_specs,
        out_shape=out_shapes,
        compiler_params=pltpu.CompilerParams(
            dimension_semantics=("parallel",), vmem_limit_bytes=56 * MIB),
        name="proj_prep",
    )(x, g, w, *tables, gq, gkv, gcq, gck, wuq, wukv, _stream_select(bm))


def _stream_select(bm):
    blocks = []
    for d in B_DILATIONS[1:]:
        out_row = jnp.arange(bm)
        src = (out_row % (bm // d)) * d + out_row // (bm // d)
        blocks.append(jax.nn.one_hot(src, bm, dtype=BF16))
    return jnp.concatenate(blocks, axis=0)


ATTN_KC = 1024
ATTN_SUB = 256


def _attn_kernel(*refs, group, dk, dv, bq, n_blk, n_side, sub):
    q_ref, k_ref, v_ref = refs[0:3]
    side_in = refs[3:3 + n_side]
    o_ref = refs[3 + n_side]
    side_out = refs[4 + n_side:4 + 2 * n_side]
    vt_scr = refs[4 + 2 * n_side]

    for src, dst in zip(side_in, side_out):
        dst[...] = src[...].astype(BF16)

    @pl.when(pl.program_id(2) == 0)
    def _():
        vt_scr[...] = v_ref[0].T

    q_blocks = [jnp.concatenate([q_ref[0, i * bq:(i + 1) * bq, g * dk:(g + 1) * dk]
                                 for g in range(group)], axis=0) for i in range(n_blk)]
    n_chunks = SEQ // ATTN_KC
    n_sub = ATTN_KC // sub

    def scores(item, j):
        blk, c = item
        r0 = c * ATTN_KC + j * sub
        return lax.dot_general(k_ref[0, r0:r0 + sub, :], q_blocks[blk], (((1,), (1,)), ((), ())),
                               preferred_element_type=F32)

    def fold8(x, op):
        return op(x.reshape(x.shape[0] // 8, 8, x.shape[1]), axis=0)

    def col_max(pieces):
        mx = fold8(pieces[0], jnp.max)
        for piece in pieces[1:]:
            mx = jnp.maximum(mx, fold8(piece, jnp.max))
        return jnp.max(mx, axis=0, keepdims=True)

    items = [(blk, c) for blk in range(n_blk) for c in range(n_chunks)]
    s_cur = [scores(items[0], j) for j in range(n_sub)]
    m = l = acc = None
    for idx, (blk, c) in enumerate(items):
        m_c = col_max(s_cur)
        m_new = m_c if c == 0 else jnp.maximum(m, m_c)
        s_next, p_sum, p_bf = [], None, []
        for j in range(n_sub):
            if idx + 1 < len(items):
                s_next.append(scores(items[idx + 1], j))
            p = jnp.exp2(s_cur[j] - m_new)
            p_sum = fold8(p, jnp.sum) if j == 0 else p_sum + fold8(p, jnp.sum)
            p_bf.append(p.astype(BF16))
        l_c = jnp.sum(p_sum, axis=0, keepdims=True)
        pv = jnp.dot(vt_scr[:, c * ATTN_KC:(c + 1) * ATTN_KC], jnp.concatenate(p_bf, axis=0),
                     preferred_element_type=F32)
        if c == 0:
            l, acc = l_c, pv
        else:
            alpha = jnp.exp2(m - m_new)
            l = alpha * l + l_c
            acc = alpha * acc + pv
        m = m_new
        s_cur = s_next
        if c == n_chunks - 1:
            out = acc / l
            for g in range(group):
                o_ref[0, blk * bq:(blk + 1) * bq, g * dv:(g + 1) * dv] = (
                    out[:, g * bq:(g + 1) * bq].T)


def _attention(q, k, v, *, kv_heads, group, dk, dv, bq, name, side_casts=(), sub=ATTN_SUB,
               n_blk=1):
    b = q.shape[0]
    rows_q = n_blk * bq
    n_q = SEQ // rows_q

    def side_block(rows_total, rows):
        last = rows_total // rows - 1
        return lambda bi, h, qi: jnp.minimum((bi * kv_heads + h) * n_q + qi, last)

    in_specs = [
        pl.BlockSpec((1, rows_q, group * dk), lambda bi, h, qi: (bi, qi, h)),
        pl.BlockSpec((1, SEQ, dk), lambda bi, h, qi: (bi, 0, h)),
        pl.BlockSpec((1, SEQ, dv), lambda bi, h, qi: (bi, 0, h)),
    ]
    out_specs = [pl.BlockSpec((1, rows_q, group * dv), lambda bi, h, qi: (bi, qi, h))]
    out_shapes = [jax.ShapeDtypeStruct((b, SEQ, kv_heads * group * dv), F32)]
    operands = [q, k, v]
    for w, layer, rows in side_casts:
        _, r, c = w.shape
        assert r % rows == 0 and b * kv_heads * n_q >= r // rows
        blk = side_block(r, rows)
        in_specs.append(pl.BlockSpec(
            (None, rows, c), lambda bi, h, qi, blk=blk, layer=layer: (layer, blk(bi, h, qi), 0)))
        out_specs.append(pl.BlockSpec((rows, c), lambda bi, h, qi, blk=blk: (blk(bi, h, qi), 0)))
        out_shapes.append(jax.ShapeDtypeStruct((r, c), BF16))
        operands.append(w)
    return pl.pallas_call(
        functools.partial(_attn_kernel, group=group, dk=dk, dv=dv, bq=bq, n_blk=n_blk,
                          n_side=len(side_casts), sub=sub),
        grid=(b, kv_heads, n_q),
        in_specs=in_specs,
        out_specs=out_specs,
        out_shape=out_shapes,
        scratch_shapes=[pltpu.VMEM((dv, SEQ), BF16)],
        compiler_params=pltpu.CompilerParams(
            dimension_semantics=("arbitrary", "arbitrary", "arbitrary"),
            vmem_limit_bytes=48 * MIB),
        name=name,
    )(*operands)


DIL_BQ = 128
DIL_WIN = DIL_BQ + 2 * B_HALF


def _dil_block(q_ref, k_ref, v_ref, bias_ref, stream, q0, length):
    if isinstance(q0, int):
        start = min(max(q0 - B_HALF, 0), length - DIL_WIN)
        case = (q0 - start) // B_HALF
    else:
        start = pl.multiple_of(jnp.clip(q0 - B_HALF, 0, length - DIL_WIN), B_HALF)
        case = lax.shift_right_logical(q0 - start, 6)
    q = q_ref[0, pl.ds(q0, DIL_BQ), stream * 128:(stream + 1) * 128]
    k = k_ref[0, pl.ds(start, DIL_WIN), stream * 128:(stream + 1) * 128]
    vx = v_ref[0, pl.ds(start, DIL_WIN), stream * 256:(stream + 1) * 256]
    s = lax.dot_general(q, k, (((1,), (1,)), ((), ())), preferred_element_type=F32)
    s = s + bias_ref[case]
    m = jnp.max(s, axis=-1, keepdims=True)
    p = jnp.exp2(s - m).astype(BF16)
    ol = jnp.dot(p, vx, preferred_element_type=F32)
    l = ol[:, 128:256]
    return ol[:, 0:128] / l, m + jnp.log(l) * LOG2E


def _dil_kernel(q1, k1, v1, q4, k4, v4, q16, k16, v16, bias_ref, o_ref,
                osm4, lsm4, osm16, lsm16, otok, ltok):
    def body1(n, carry):
        q0 = pl.multiple_of(n * DIL_BQ, DIL_BQ)
        o, lse = _dil_block(q1, k1, v1, bias_ref, 0, q0, SEQ)
        otok[0, pl.ds(q0, DIL_BQ), :] = o
        ltok[0, pl.ds(q0, DIL_BQ), :] = lse
        return carry

    lax.fori_loop(0, SEQ // DIL_BQ, body1, 0, unroll=16)

    def body4(n, carry):
        q0 = pl.multiple_of(n * DIL_BQ, DIL_BQ)
        for r in range(4):
            o, lse = _dil_block(q4, k4, v4, bias_ref, r, q0, SEQ // 4)
            osm4[pl.ds(q0, DIL_BQ), r * 128:(r + 1) * 128] = o
            lsm4[pl.ds(q0, DIL_BQ), r * 128:(r + 1) * 128] = lse
        return carry

    lax.fori_loop(0, SEQ // 4 // DIL_BQ, body4, 0, unroll=4)
    for r in range(4):
        otok[1, pl.ds(r, SEQ // 4, stride=4), :] = osm4[:, r * 128:(r + 1) * 128]
        ltok[1, pl.ds(r, SEQ // 4, stride=4), :] = lsm4[:, r * 128:(r + 1) * 128]

    for n in range(SEQ // 16 // DIL_BQ):
        q0 = n * DIL_BQ
        for r in range(16):
            o, lse = _dil_block(q16, k16, v16, bias_ref, r, q0, SEQ // 16)
            osm16[q0:q0 + DIL_BQ, r * 128:(r + 1) * 128] = o
            lsm16[q0:q0 + DIL_BQ, r * 128:(r + 1) * 128] = lse
    for r in range(16):
        otok[2, pl.ds(r, SEQ // 16, stride=16), :] = osm16[:, r * 128:(r + 1) * 128]
        ltok[2, pl.ds(r, SEQ // 16, stride=16), :] = lsm16[:, r * 128:(r + 1) * 128]

    chunk = 256

    def combine(c, carry):
        r0 = pl.multiple_of(c * chunk, chunk)
        l0 = ltok[0, pl.ds(r0, chunk), :]
        l1 = ltok[1, pl.ds(r0, chunk), :]
        l2 = ltok[2, pl.ds(r0, chunk), :]
        mx = jnp.maximum(jnp.maximum(l0, l1), l2)
        e0 = jnp.exp2(l0 - mx)
        e1 = jnp.exp2(l1 - mx)
        e2 = jnp.exp2(l2 - mx)
        num = (e0 * otok[0, pl.ds(r0, chunk), :] + e1 * otok[1, pl.ds(r0, chunk), :]
               + e2 * otok[2, pl.ds(r0, chunk), :])
        o_ref[0, pl.ds(r0, chunk), :] = num / (e0 + e1 + e2)
        return carry

    lax.fori_loop(0, SEQ // chunk, combine, 0)


def _dil_bias():
    i = jnp.arange(DIL_BQ)[:, None]
    j = jnp.arange(DIL_WIN)[None, :]
    return jnp.stack([jnp.where(jnp.abs(j - i - c * B_HALF) <= B_HALF, 0.0, NEG)
                      for c in range(3)]).astype(F32)


def _dilated(qkv_by_dilation):
    b = qkv_by_dilation[0].shape[0]
    in_specs = []
    for d in B_DILATIONS:
        in_specs += [pl.BlockSpec((1, SEQ // d, d * 128), lambda bi, h: (bi, 0, h))] * 2
        in_specs += [pl.BlockSpec((1, SEQ // d, d * 256), lambda bi, h: (bi, 0, h))]
    in_specs += [pl.BlockSpec((3, DIL_BQ, DIL_WIN), lambda bi, h: (0, 0, 0))]
    qkv_by_dilation = list(qkv_by_dilation) + [_dil_bias()]
    return pl.pallas_call(
        _dil_kernel,
        grid=(b, B_HEADS),
        in_specs=in_specs,
        out_specs=pl.BlockSpec((1, SEQ, 128), lambda bi, h: (bi, 0, h)),
        out_shape=jax.ShapeDtypeStruct((b, SEQ, B_WIDTH), F32),
        scratch_shapes=[
            pltpu.VMEM((SEQ // 4, 4 * 128), F32), pltpu.VMEM((SEQ // 4, 4 * 128), F32),
            pltpu.VMEM((SEQ // 16, 16 * 128), F32), pltpu.VMEM((SEQ // 16, 16 * 128), F32),
            pltpu.VMEM((3, SEQ, 128), F32), pltpu.VMEM((3, SEQ, 128), F32),
        ],
        compiler_params=pltpu.CompilerParams(
            dimension_semantics=("parallel", "parallel"), vmem_limit_bytes=56 * MIB),
        name="dilated_attn",
    )(*qkv_by_dilation)


def _out_proj_kernel(ya, yb, yc, x_ref, g_ref, w_ref, o_ref):
    b0, c0 = A_WIDTH, A_WIDTH + B_WIDTH
    na = _rms(ya[...], g_ref[:, 0:b0]).astype(BF16)
    nb = _rms(yb[...], g_ref[:, b0:c0]).astype(BF16)
    nc = _rms(yc[...], g_ref[:, c0:]).astype(BF16)
    y = (jnp.dot(na, w_ref[0:b0, :], preferred_element_type=F32)
         + jnp.dot(nb, w_ref[b0:c0, :], preferred_element_type=F32)
         + jnp.dot(nc, w_ref[c0:, :], preferred_element_type=F32))
    o_ref[...] = x_ref[...] + y


def _out_proj(ya, yb, yc, x, g, w):
    t = x.shape[0]
    bm = 512
    row = lambda i: (i, 0)
    return pl.pallas_call(
        _out_proj_kernel,
        grid=(t // bm,),
        in_specs=[
            pl.BlockSpec((bm, A_WIDTH), row),
            pl.BlockSpec((bm, B_WIDTH), row),
            pl.BlockSpec((bm, C_WIDTH), row),
            pl.BlockSpec((bm, D_MODEL), row),
            _resident((1, D_MODEL)),
            _resident((D_MODEL, D_MODEL)),
        ],
        out_specs=pl.BlockSpec((bm, D_MODEL), row),
        out_shape=jax.ShapeDtypeStruct((t, D_MODEL), F32),
        compiler_params=pltpu.CompilerParams(
            dimension_semantics=("parallel",), vmem_limit_bytes=48 * MIB),
        name="out_proj",
    )(ya, yb, yc, x, g, w)


FFN_BF = 512


def _ffn_kernel(x_ref, g_ref, wg_ref, wu_ref, wd_ref, fg_ref, o_ref, h_scr, *, final_norm):
    j = pl.program_id(1)

    @pl.when(j == 0)
    def _():
        x = x_ref[...]
        h_scr[...] = _rms(x, g_ref[...]).astype(BF16)
        o_ref[...] = x

    h = h_scr[...]
    gate = jnp.dot(h, wg_ref[...], preferred_element_type=F32)
    up = jnp.dot(h, wu_ref[...], preferred_element_type=F32)
    ff = (gate * jax.nn.sigmoid(gate)) * up
    o_ref[...] += jnp.dot(ff.astype(BF16), wd_ref[...], preferred_element_type=F32)

    if final_norm:
        @pl.when(j == pl.num_programs(1) - 1)
        def _():
            o_ref[...] = _rms(o_ref[...], fg_ref[...])


def _ffn(x, g, wg, wu, wd, fg, *, final_norm):
    t = x.shape[0]
    bm, bf = 1024, FFN_BF
    return pl.pallas_call(
        functools.partial(_ffn_kernel, final_norm=final_norm),
        grid=(t // bm, D_FF // bf),
        in_specs=[
            pl.BlockSpec((bm, D_MODEL), lambda i, j: (i, 0)),
            pl.BlockSpec((1, D_MODEL), lambda i, j: (0, 0)),
            pl.BlockSpec((D_MODEL, bf), lambda i, j: (0, j)),
            pl.BlockSpec((D_MODEL, bf), lambda i, j: (0, j)),
            pl.BlockSpec((bf, D_MODEL), lambda i, j: (j, 0)),
            pl.BlockSpec((1, D_MODEL), lambda i, j: (0, 0)),
        ],
        out_specs=pl.BlockSpec((bm, D_MODEL), lambda i, j: (i, 0)),
        out_shape=jax.ShapeDtypeStruct((t, D_MODEL), F32),
        scratch_shapes=[pltpu.VMEM((bm, D_MODEL), BF16)],
        compiler_params=pltpu.CompilerParams(
            dimension_semantics=("parallel", "arbitrary"), vmem_limit_bytes=60 * MIB),
        name="swiglu_ffn",
    )(x, g, wg, wu, wd, fg)


def _rope_tables():
    pos = jnp.arange(SEQ, dtype=jnp.int32)

    def angles(p, dim):
        inv = ROPE_THETA ** (-jnp.arange(0, dim, 2, dtype=F32) / dim)
        return p.astype(F32)[:, None] * inv[None, :]

    ang_b = angles(pos, HEAD_DIM)
    ang_a = angles(pos, A_ROPE)
    ang_g = angles(jnp.arange(GRID_W, dtype=jnp.int32), HEAD_DIM // 2)
    cos_g, sin_g = jnp.cos(ang_g), jnp.sin(ang_g)
    cos_r, sin_r = (jnp.repeat(a, GRID_W, axis=0) for a in (cos_g, sin_g))
    cos_k, sin_k = (jnp.tile(a, (SEQ // GRID_W, 1)) for a in (cos_g, sin_g))
    cos_pa, sin_pa = jnp.cos(ang_a), jnp.sin(ang_a)
    cos_pb, sin_pb = jnp.cos(ang_b), jnp.sin(ang_b)
    z = jnp.zeros((SEQ, 32), F32)
    cat = lambda *xs: jnp.concatenate(xs, axis=-1)
    cos_a = cat(cos_pa, z, cos_pa, z)
    sin_a = cat(-sin_pa, z, sin_pa, z)
    cos_b = cat(cos_pb, cos_pb)
    sin_b = cat(-sin_pb, sin_pb)
    cos_c = cat(cos_r, cos_k, cos_r, cos_k)
    sin_c = cat(-sin_r, -sin_k, sin_r, sin_k)
    return cos_a, sin_a, cos_b, sin_b, cos_c, sin_c


def _axial_perm():
    a = jnp.arange(32)
    return jnp.concatenate([a, a + 64, a + 32, a + 96])


def _w_in_layout_kernel(w_ref, o_ref):
    bl = w_ref.shape[1]

    def put(col, rows):
        o_ref[:, col:col + 128] = rows.T.astype(BF16)

    for t in range(OFF_KR // 128):
        put(t * 128, w_ref[t * 128:(t + 1) * 128, :])
    z = jnp.zeros((32, bl), F32)
    put(OFF_KR, jnp.concatenate([w_ref[1024:1056, :], z, w_ref[1056:1088, :], z], axis=0))
    for t in range(3 * B_WIDTH // 128):
        put(OFF_BQ + t * 128, w_ref[1088 + t * 128:1088 + (t + 1) * 128, :])
    for h in range(C_HEADS + C_KV_HEADS):
        r0 = 3392 + h * 128
        put(OFF_CQH + h * 128, jnp.concatenate(
            [w_ref[r0:r0 + 32, :], w_ref[r0 + 64:r0 + 96, :],
             w_ref[r0 + 32:r0 + 64, :], w_ref[r0 + 96:r0 + 128, :]], axis=0))
    for t in range(C_KV_WIDTH // 128):
        put(OFF_CV + t * 128, w_ref[4416 + t * 128:4416 + (t + 1) * 128, :])


def _layout_w_in(w, layer):
    wt = jnp.swapaxes(w, 1, 2)
    bl = 256
    width = wt.shape[1]
    return pl.pallas_call(
        _w_in_layout_kernel,
        grid=(D_MODEL // bl,),
        in_specs=[pl.BlockSpec((None, width, bl), lambda i: (layer, 0, i))],
        out_specs=pl.BlockSpec((bl, PROJ_WIDTH), lambda i: (i, 0)),
        out_shape=jax.ShapeDtypeStruct((D_MODEL, PROJ_WIDTH), BF16),
        compiler_params=pltpu.CompilerParams(
            dimension_semantics=("parallel",), vmem_limit_bytes=40 * MIB),
        name="w_in_layout",
    )(wt)


def _layout_w_uq(w):
    w = w.reshape(A_RANK, A_HEADS, A_NOPE + A_ROPE)
    z = jnp.zeros((A_RANK, A_HEADS, 32), w.dtype)
    out = jnp.concatenate([w[:, :, :128], w[:, :, 128:160], z, w[:, :, 160:192], z], axis=2)
    return out.reshape(A_RANK, A_HEADS * A_DK).astype(BF16)


def _layout_w_ukv(w):
    w = w.reshape(A_RANK, A_HEADS, 2 * HEAD_DIM)
    out = jnp.concatenate([w[:, :, :128].reshape(A_RANK, A_WIDTH),
                           w[:, :, 128:].reshape(A_RANK, A_WIDTH)], axis=1)
    return out.astype(BF16)


def kernel(x, attn_norm, w_in, a_q_norm, a_w_uq, a_kv_norm, a_w_ukv, c_q_norm, c_k_norm,
           out_norm, w_out, ffn_norm, w_gate, w_up, w_down, final_norm):
    bsz, seq, _ = x.shape
    t = bsz * seq
    depth = w_in.shape[0]
    tables = _rope_tables()
    perm = _axial_perm()
    xf = x.reshape(t, D_MODEL)
    for l in range(depth):
        (qa, ka, va, qb1, kb1, vb1, qb4, kb4, vb4, qb16, kb16, vb16, qc, kc, vc) = _proj_prep(
            xf, attn_norm[l][None, :], _layout_w_in(w_in, l), tables,
            a_q_norm[l][None, :], a_kv_norm[l][None, :],
            c_q_norm[l][perm][None, :], c_k_norm[l][perm][None, :],
            _layout_w_uq(a_w_uq[l]), _layout_w_ukv(a_w_ukv[l]))
        r3 = lambda a, n=1: a.reshape(bsz, seq // n, a.shape[1])
        nb = 2 if l == 0 else 4
        ya, wd_bf, wo_bf = _attention(
            r3(qa), r3(ka), r3(va), kv_heads=A_HEADS, group=1, dk=A_DK, dv=HEAD_DIM, bq=512,
            name="mla_attn", side_casts=((w_down, l, 176), (w_out, l, 16 * nb)),
            sub=256, n_blk=nb)
        yb = _dilated([r3(qb1), r3(kb1), r3(vb1), r3(qb4, 4), r3(kb4, 4), r3(vb4, 4),
                       r3(qb16, 16), r3(kb16, 16), r3(vb16, 16)])
        yc, wg_bf, wu_bf = _attention(
            r3(qc), r3(kc), r3(vc), kv_heads=C_KV_HEADS, group=C_GROUP, dk=HEAD_DIM, dv=HEAD_DIM,
            bq=256, name="gqa_attn", side_casts=((w_gate, l, 16 * nb), (w_up, l, 16 * nb)),
            sub=1024, n_blk=nb)
        xf = _out_proj(ya.reshape(t, A_WIDTH), yb.reshape(t, B_WIDTH), yc.reshape(t, C_WIDTH),
                       xf, out_norm[l][None, :], wo_bf)
        xf = _ffn(xf, ffn_norm[l][None, :], wg_bf, wu_bf, wd_bf, final_norm[None, :],
                  final_norm=(l == depth - 1))
    return xf.reshape(bsz, seq, D_MODEL)
```

```python
import functools

import jax
import jax.numpy as jnp
from jax import lax
from jax.experimental import pallas as pl
from jax.experimental.pallas import tpu as pltpu

F32 = jnp.float32
BF16 = jnp.bfloat16

D_MODEL = 2048
SEQ = 4096
HEAD_DIM = 128
ROPE_THETA = 10000.0
GRID_W = 64
EPS = 1e-6
NEG = -1e30

A_HEADS = 4
A_RANK = 512
A_NOPE = 128
A_ROPE = 64
A_DK = 256
B_HEADS = 6
B_DILATIONS = (1, 4, 16)
B_HALF = 64
C_HEADS = 6
C_KV_HEADS = 2
C_GROUP = C_HEADS // C_KV_HEADS
A_WIDTH = A_HEADS * HEAD_DIM
B_WIDTH = B_HEADS * HEAD_DIM
C_WIDTH = C_HEADS * HEAD_DIM
C_KV_WIDTH = C_KV_HEADS * HEAD_DIM
D_FF = 5632

OFF_CQ = 0
OFF_CKV = 512
OFF_KR = 1024
OFF_BQ = 1152
OFF_BK = OFF_BQ + B_WIDTH
OFF_BV = OFF_BK + B_WIDTH
OFF_CQH = OFF_BV + B_WIDTH
OFF_CK = OFF_CQH + C_WIDTH
OFF_CV = OFF_CK + C_KV_WIDTH
PROJ_USED = OFF_CV + C_KV_WIDTH
PROJ_WIDTH = PROJ_USED

LOG2E = 1.4426950408889634
SCALE_A = (A_NOPE + A_ROPE) ** -0.5 * LOG2E
SCALE_B = HEAD_DIM ** -0.5 * LOG2E
SCALE_C = HEAD_DIM ** -0.5 * LOG2E

MIB = 1024 * 1024


def _rms(x, g):
    ms = jnp.mean(x * x, axis=-1, keepdims=True)
    return x * lax.rsqrt(ms + EPS) * g


def _rope(x, c, s):
    return x * c + pltpu.roll(x, 64, 1) * s


def _resident(shape):
    return pl.BlockSpec(shape, lambda i: (0,) * len(shape), pipeline_mode=pl.Buffered(1))


PREP_BM = 256


def _proj_prep_kernel(x_ref, g_ref, w_ref, cos_a, sin_a, cos_b, sin_b, cos_c, sin_c,
                      gq, gkv, gcq, gck, wuq, wukv, sel_ref,
                      qa, ka, va, qb1, kb1, vb1, qb4, kb4, vb4, qb16, kb16, vb16, qc, kc, vc):
    bm = PREP_BM
    hn = _rms(x_ref[...], g_ref[...]).astype(BF16)

    def project(lo, hi):
        return jnp.dot(hn, w_ref[:, lo:hi], preferred_element_type=F32)

    seg_a = project(0, OFF_BQ)
    seg_bq = project(OFF_BQ, OFF_BK)

    ca, sa = cos_a[...], sin_a[...]
    cq = _rms(seg_a[:, OFF_CQ:OFF_CQ + A_RANK], gq[...]).astype(BF16)
    q = jnp.dot(cq, wuq[...], preferred_element_type=F32)
    for h in range(A_HEADS):
        lo = h * A_DK
        qa[:, lo:lo + 128] = (q[:, lo:lo + 128] * SCALE_A).astype(BF16)
        qa[:, lo + 128:lo + 256] = (_rope(q[:, lo + 128:lo + 256], ca, sa) * SCALE_A).astype(BF16)
    ckv = _rms(seg_a[:, OFF_CKV:OFF_CKV + A_RANK], gkv[...]).astype(BF16)
    kv = jnp.dot(ckv, wukv[...], preferred_element_type=F32)
    kr = _rope(seg_a[:, OFF_KR:OFF_KR + 128], ca, sa).astype(BF16)
    for h in range(A_HEADS):
        lo = h * A_DK
        ka[:, lo:lo + 128] = kv[:, h * 128:(h + 1) * 128].astype(BF16)
        ka[:, lo + 128:lo + 256] = kr
    va[...] = kv[:, A_WIDTH:2 * A_WIDTH].astype(BF16)

    cb, sb = cos_b[...], sin_b[...]
    sel = sel_ref[...]
    ones = jnp.ones((bm, 128), BF16)

    def mixer_b(seg, scale, use_rope, is_v, o1, o4, o16):
        width = 256 if is_v else 128
        tiles = []
        for h in range(B_HEADS):
            x = seg[:, h * 128:(h + 1) * 128]
            if use_rope:
                x = _rope(x, cb, sb)
            if scale is not None:
                x = x * scale
            tiles.append(x.astype(BF16))
            o1[:, h * width:h * width + 128] = tiles[h]
            if is_v:
                o1[:, h * width + 128:(h + 1) * width] = ones
        y = jnp.dot(sel, jnp.concatenate(tiles, axis=1), preferred_element_type=F32)
        for d, od, base in ((4, o4, 0), (16, o16, bm)):
            n = bm // d
            for h in range(B_HEADS):
                for r in range(d):
                    c0 = (h * d + r) * width
                    rows = y[base + r * n:base + (r + 1) * n, h * 128:(h + 1) * 128]
                    od[:, c0:c0 + 128] = rows.astype(BF16)
                    if is_v:
                        od[:, c0 + 128:c0 + 256] = ones[0:n]

    seg_bk = project(OFF_BK, OFF_BV)
    mixer_b(seg_bq, SCALE_B, True, False, qb1, qb4, qb16)
    seg_bv = project(OFF_BV, OFF_CQH)
    mixer_b(seg_bk, None, True, False, kb1, kb4, kb16)
    seg_c = project(OFF_CQH, PROJ_WIDTH)
    mixer_b(seg_bv, None, False, True, vb1, vb4, vb16)

    cc, sc = cos_c[...], sin_c[...]
    for h in range(C_HEADS):
        x = _rms(seg_c[:, h * 128:(h + 1) * 128], gcq[...])
        qc[:, h * 128:(h + 1) * 128] = (_rope(x, cc, sc) * SCALE_C).astype(BF16)
    for h in range(C_KV_HEADS):
        lo = C_WIDTH + h * 128
        x = _rms(seg_c[:, lo:lo + 128], gck[...])
        kc[:, h * 128:(h + 1) * 128] = _rope(x, cc, sc).astype(BF16)
    vc[...] = seg_c[:, C_WIDTH + C_KV_WIDTH:].astype(BF16)


def _proj_prep(x, g, w, tables, gq, gkv, gcq, gck, wuq, wukv):
    t = x.shape[0]
    bm = PREP_BM
    nblk_seq = SEQ // bm
    row = lambda i: (i, 0)
    tab = lambda i: (i % nblk_seq, 0)
    out_widths = [A_HEADS * A_DK, A_HEADS * A_DK, A_WIDTH]
    out_shapes = [jax.ShapeDtypeStruct((t, w_), BF16) for w_ in out_widths]
    out_specs = [pl.BlockSpec((bm, w_), row) for w_ in out_widths]
    for d in B_DILATIONS:
        for w_ in (B_WIDTH, B_WIDTH, 2 * B_WIDTH):
            out_shapes.append(jax.ShapeDtypeStruct((t // d, d * w_), BF16))
            out_specs.append(pl.BlockSpec((bm // d, d * w_), row))
    for w_ in (C_WIDTH, C_KV_WIDTH, C_KV_WIDTH):
        out_shapes.append(jax.ShapeDtypeStruct((t, w_), BF16))
        out_specs.append(pl.BlockSpec((bm, w_), row))
    in_specs = [pl.BlockSpec((bm, D_MODEL), row), _resident((1, D_MODEL)),
                _resident((D_MODEL, PROJ_WIDTH))]
    in_specs += [pl.BlockSpec((bm, 128), tab)] * 6
    in_specs += [_resident((1, A_RANK)), _resident((1, A_RANK)),
                 _resident((1, 128)), _resident((1, 128)),
                 _resident((A_RANK, A_HEADS * A_DK)), _resident((A_RANK, 2 * A_WIDTH)),
                 _resident((2 * bm, bm))]
    return pl.pallas_call(
        _proj_prep_kernel,
        grid=(t // bm,),
        in_specs=in_specs,
        out_specs=out_specs,
        out_shape=out_shapes,
        compiler_params=pltpu.CompilerParams(
            dimension_semantics=("parallel",), vmem_limit_bytes=56 * MIB),
        name="proj_prep",
    )(x, g, w, *tables, gq, gkv, gcq, gck, wuq, wukv, _stream_select(bm))


def _stream_select(bm):
    blocks = []
    for d in B_DILATIONS[1:]:
        out_row = jnp.arange(bm)
        src = (out_row % (bm // d)) * d + out_row // (bm // d)
        blocks.append(jax.nn.one_hot(src, bm, dtype=BF16))
    return jnp.concatenate(blocks, axis=0)


ATTN_KC = 1024
ATTN_SUB = 256


def _attn_kernel(*refs, group, dk, dv, bq, n_blk, n_side, kc, sub):
    q_ref, k_ref, v_ref = refs[0:3]
    side_in = refs[3:3 + n_side]
    o_ref = refs[3 + n_side]
    side_out = refs[4 + n_side:4 + 2 * n_side]
    vt_scr = refs[4 + 2 * n_side]

    for src, dst in zip(side_in, side_out):
        dst[...] = src[...].astype(BF16)

    @pl.when(pl.program_id(2) == 0)
    def _():
        vt_scr[...] = v_ref[0].T

    q_blocks = [jnp.concatenate([q_ref[0, i * bq:(i + 1) * bq, g * dk:(g + 1) * dk]
                                 for g in range(group)], axis=0) for i in range(n_blk)]
    n_chunks = SEQ // kc
    n_sub = kc // sub

    def scores(item, j):
        blk, c = item
        r0 = c * kc + j * sub
        return lax.dot_general(k_ref[0, r0:r0 + sub, :], q_blocks[blk], (((1,), (1,)), ((), ())),
                               preferred_element_type=F32)

    def fold8(x, op):
        return op(x.reshape(x.shape[0] // 8, 8, x.shape[1]), axis=0)

    def col_max(pieces):
        mx = fold8(pieces[0], jnp.max)
        for piece in pieces[1:]:
            mx = jnp.maximum(mx, fold8(piece, jnp.max))
        return jnp.max(mx, axis=0, keepdims=True)

    items = [(blk, c) for blk in range(n_blk) for c in range(n_chunks)]
    s_cur = [scores(items[0], j) for j in range(n_sub)]
    m = l = acc = None
    for idx, (blk, c) in enumerate(items):
        m_c = col_max(s_cur)
        m_new = m_c if c == 0 else jnp.maximum(m, m_c)
        s_next, p_sum, p_bf = [], None, []
        for j in range(n_sub):
            if idx + 1 < len(items):
                s_next.append(scores(items[idx + 1], j))
            p = jnp.exp2(s_cur[j] - m_new)
            p_sum = fold8(p, jnp.sum) if j == 0 else p_sum + fold8(p, jnp.sum)
            p_bf.append(p.astype(BF16))
        l_c = jnp.sum(p_sum, axis=0, keepdims=True)
        pv = jnp.dot(vt_scr[:, c * kc:(c + 1) * kc], jnp.concatenate(p_bf, axis=0),
                     preferred_element_type=F32)
        if c == 0:
            l, acc = l_c, pv
        else:
            alpha = jnp.exp2(m - m_new)
            l = alpha * l + l_c
            acc = alpha * acc + pv
        m = m_new
        s_cur = s_next
        if c == n_chunks - 1:
            out = acc / l
            for g in range(group):
                o_ref[0, blk * bq:(blk + 1) * bq, g * dv:(g + 1) * dv] = (
                    out[:, g * bq:(g + 1) * bq].T)


def _attention(q, k, v, *, kv_heads, group, dk, dv, bq, name, side_casts=(), kc=ATTN_KC,
               sub=ATTN_SUB, n_blk=1):
    b = q.shape[0]
    rows_q = n_blk * bq
    n_q = SEQ // rows_q

    def side_block(rows_total, rows):
        last = rows_total // rows - 1
        return lambda bi, h, qi: jnp.minimum((bi * kv_heads + h) * n_q + qi, last)

    in_specs = [
        pl.BlockSpec((1, rows_q, group * dk), lambda bi, h, qi: (bi, qi, h)),
        pl.BlockSpec((1, SEQ, dk), lambda bi, h, qi: (bi, 0, h)),
        pl.BlockSpec((1, SEQ, dv), lambda bi, h, qi: (bi, 0, h)),
    ]
    out_specs = [pl.BlockSpec((1, rows_q, group * dv), lambda bi, h, qi: (bi, qi, h))]
    out_shapes = [jax.ShapeDtypeStruct((b, SEQ, kv_heads * group * dv), F32)]
    operands = [q, k, v]
    for w, layer, rows in side_casts:
        _, r, c = w.shape
        assert r % rows == 0 and b * kv_heads * n_q >= r // rows
        blk = side_block(r, rows)
        in_specs.append(pl.BlockSpec(
            (None, rows, c), lambda bi, h, qi, blk=blk, layer=layer: (layer, blk(bi, h, qi), 0)))
        out_specs.append(pl.BlockSpec((rows, c), lambda bi, h, qi, blk=blk: (blk(bi, h, qi), 0)))
        out_shapes.append(jax.ShapeDtypeStruct((r, c), BF16))
        operands.append(w)
    return pl.pallas_call(
        functools.partial(_attn_kernel, group=group, dk=dk, dv=dv, bq=bq, n_blk=n_blk,
                          n_side=len(side_casts), kc=kc, sub=sub),
        grid=(b, kv_heads, n_q),
        in_specs=in_specs,
        out_specs=out_specs,
        out_shape=out_shapes,
        scratch_shapes=[pltpu.VMEM((dv, SEQ), BF16)],
        compiler_params=pltpu.CompilerParams(
            dimension_semantics=("arbitrary", "arbitrary", "arbitrary"),
            vmem_limit_bytes=48 * MIB),
        name=name,
    )(*operands)


DIL_BQ = 128
DIL_WIN = DIL_BQ + 2 * B_HALF


def _dil_block(q_ref, k_ref, v_ref, bias_ref, stream, q0, length):
    if isinstance(q0, int):
        start = min(max(q0 - B_HALF, 0), length - DIL_WIN)
        case = (q0 - start) // B_HALF
    else:
        start = pl.multiple_of(jnp.clip(q0 - B_HALF, 0, length - DIL_WIN), B_HALF)
        case = lax.shift_right_logical(q0 - start, 6)
    q = q_ref[0, pl.ds(q0, DIL_BQ), stream * 128:(stream + 1) * 128]
    k = k_ref[0, pl.ds(start, DIL_WIN), stream * 128:(stream + 1) * 128]
    vx = v_ref[0, pl.ds(start, DIL_WIN), stream * 256:(stream + 1) * 256]
    s = lax.dot_general(q, k, (((1,), (1,)), ((), ())), preferred_element_type=F32)
    s = s + bias_ref[case]
    m = jnp.max(s, axis=-1, keepdims=True)
    p = jnp.exp2(s - m).astype(BF16)
    ol = jnp.dot(p, vx, preferred_element_type=F32)
    l = ol[:, 128:256]
    return ol[:, 0:128] / l, m + jnp.log(l) * LOG2E


def _dil_kernel(q1, k1, v1, q4, k4, v4, q16, k16, v16, bias_ref, o_ref,
                osm4, lsm4, osm16, lsm16, otok, ltok):
    def body1(n, carry):
        q0 = pl.multiple_of(n * DIL_BQ, DIL_BQ)
        o, lse = _dil_block(q1, k1, v1, bias_ref, 0, q0, SEQ)
        otok[0, pl.ds(q0, DIL_BQ), :] = o
        ltok[0, pl.ds(q0, DIL_BQ), :] = lse
        return carry

    lax.fori_loop(0, SEQ // DIL_BQ, body1, 0, unroll=True)

    def body4(n, carry):
        q0 = pl.multiple_of(n * DIL_BQ, DIL_BQ)
        for r in range(4):
            o, lse = _dil_block(q4, k4, v4, bias_ref, r, q0, SEQ // 4)
            osm4[pl.ds(q0, DIL_BQ), r * 128:(r + 1) * 128] = o
            lsm4[pl.ds(q0, DIL_BQ), r * 128:(r + 1) * 128] = lse
        return carry

    lax.fori_loop(0, SEQ // 4 // DIL_BQ, body4, 0, unroll=True)
    for r in range(4):
        otok[1, pl.ds(r, SEQ // 4, stride=4), :] = osm4[:, r * 128:(r + 1) * 128]
        ltok[1, pl.ds(r, SEQ // 4, stride=4), :] = lsm4[:, r * 128:(r + 1) * 128]

    for n in range(SEQ // 16 // DIL_BQ):
        q0 = n * DIL_BQ
        for r in range(16):
            o, lse = _dil_block(q16, k16, v16, bias_ref, r, q0, SEQ // 16)
            osm16[q0:q0 + DIL_BQ, r * 128:(r + 1) * 128] = o
            lsm16[q0:q0 + DIL_BQ, r * 128:(r + 1) * 128] = lse
    for r in range(16):
        otok[2, pl.ds(r, SEQ // 16, stride=16), :] = osm16[:, r * 128:(r + 1) * 128]
        ltok[2, pl.ds(r, SEQ // 16, stride=16), :] = lsm16[:, r * 128:(r + 1) * 128]

    chunk = 256

    def combine(c, carry):
        r0 = pl.multiple_of(c * chunk, chunk)
        l0 = ltok[0, pl.ds(r0, chunk), :]
        l1 = ltok[1, pl.ds(r0, chunk), :]
        l2 = ltok[2, pl.ds(r0, chunk), :]
        mx = jnp.maximum(jnp.maximum(l0, l1), l2)
        e0 = jnp.exp2(l0 - mx)
        e1 = jnp.exp2(l1 - mx)
        e2 = jnp.exp2(l2 - mx)
        num = (e0 * otok[0, pl.ds(r0, chunk), :] + e1 * otok[1, pl.ds(r0, chunk), :]
               + e2 * otok[2, pl.ds(r0, chunk), :])
        o_ref[0, pl.ds(r0, chunk), :] = num / (e0 + e1 + e2)
        return carry

    lax.fori_loop(0, SEQ // chunk, combine, 0)


def _dil_bias():
    i = jnp.arange(DIL_BQ)[:, None]
    j = jnp.arange(DIL_WIN)[None, :]
    return jnp.stack([jnp.where(jnp.abs(j - i - c * B_HALF) <= B_HALF, 0.0, NEG)
                      for c in range(3)]).astype(F32)


def _dilated(qkv_by_dilation):
    b = qkv_by_dilation[0].shape[0]
    in_specs = []
    for d in B_DILATIONS:
        in_specs += [pl.BlockSpec((1, SEQ // d, d * 128), lambda bi, h: (bi, 0, h))] * 2
        in_specs += [pl.BlockSpec((1, SEQ // d, d * 256), lambda bi, h: (bi, 0, h))]
    in_specs += [pl.BlockSpec((3, DIL_BQ, DIL_WIN), lambda bi, h: (0, 0, 0))]
    qkv_by_dilation = list(qkv_by_dilation) + [_dil_bias()]
    return pl.pallas_call(
        _dil_kernel,
        grid=(b, B_HEADS),
        in_specs=in_specs,
        out_specs=pl.BlockSpec((1, SEQ, 128), lambda bi, h: (bi, 0, h)),
        out_shape=jax.ShapeDtypeStruct((b, SEQ, B_WIDTH), F32),
        scratch_shapes=[
            pltpu.VMEM((SEQ // 4, 4 * 128), F32), pltpu.VMEM((SEQ // 4, 4 * 128), F32),
            pltpu.VMEM((SEQ // 16, 16 * 128), F32), pltpu.VMEM((SEQ // 16, 16 * 128), F32),
            pltpu.VMEM((3, SEQ, 128), F32), pltpu.VMEM((3, SEQ, 128), F32),
        ],
        compiler_params=pltpu.CompilerParams(
            dimension_semantics=("parallel", "parallel"), vmem_limit_bytes=56 * MIB),
        name="dilated_attn",
    )(*qkv_by_dilation)


def _out_proj_kernel(ya, yb, yc, x_ref, g_ref, w_ref, o_ref):
    b0, c0 = A_WIDTH, A_WIDTH + B_WIDTH
    na = _rms(ya[...], g_ref[:, 0:b0]).astype(BF16)
    nb = _rms(yb[...], g_ref[:, b0:c0]).astype(BF16)
    nc = _rms(yc[...], g_ref[:, c0:]).astype(BF16)
    y = (jnp.dot(na, w_ref[0:b0, :], preferred_element_type=F32)
         + jnp.dot(nb, w_ref[b0:c0, :], preferred_element_type=F32)
         + jnp.dot(nc, w_ref[c0:, :], preferred_element_type=F32))
    o_ref[...] = x_ref[...] + y


def _out_proj(ya, yb, yc, x, g, w):
    t = x.shape[0]
    bm = 512
    row = lambda i: (i, 0)
    return pl.pallas_call(
        _out_proj_kernel,
        grid=(t // bm,),
        in_specs=[
            pl.BlockSpec((bm, A_WIDTH), row),
            pl.BlockSpec((bm, B_WIDTH), row),
            pl.BlockSpec((bm, C_WIDTH), row),
            pl.BlockSpec((bm, D_MODEL), row),
            _resident((1, D_MODEL)),
            _resident((D_MODEL, D_MODEL)),
        ],
        out_specs=pl.BlockSpec((bm, D_MODEL), row),
        out_shape=jax.ShapeDtypeStruct((t, D_MODEL), F32),
        compiler_params=pltpu.CompilerParams(
            dimension_semantics=("parallel",), vmem_limit_bytes=48 * MIB),
        name="out_proj",
    )(ya, yb, yc, x, g, w)


FFN_BF = 512


def _ffn_kernel(x_ref, g_ref, wg_ref, wu_ref, wd_ref, fg_ref, o_ref, h_scr, *, final_norm):
    j = pl.program_id(1)

    @pl.when(j == 0)
    def _():
        x = x_ref[...]
        h_scr[...] = _rms(x, g_ref[...]).astype(BF16)
        o_ref[...] = x

    h = h_scr[...]
    gate = jnp.dot(h, wg_ref[...], preferred_element_type=F32)
    up = jnp.dot(h, wu_ref[...], preferred_element_type=F32)
    ff = (gate * jax.nn.sigmoid(gate)) * up
    o_ref[...] += jnp.dot(ff.astype(BF16), wd_ref[...], preferred_element_type=F32)

    if final_norm:
        @pl.when(j == pl.num_programs(1) - 1)
        def _():
            o_ref[...] = _rms(o_ref[...], fg_ref[...])


def _ffn(x, g, wg, wu, wd, fg, *, final_norm):
    t = x.shape[0]
    bm, bf = 1024, FFN_BF
    return pl.pallas_call(
        functools.partial(_ffn_kernel, final_norm=final_norm),
        grid=(t // bm, D_FF // bf),
        in_specs=[
            pl.BlockSpec((bm, D_MODEL), lambda i, j: (i, 0)),
            pl.BlockSpec((1, D_MODEL), lambda i, j: (0, 0)),
            pl.BlockSpec((D_MODEL, bf), lambda i, j: (0, j)),
            pl.BlockSpec((D_MODEL, bf), lambda i, j: (0, j)),
            pl.BlockSpec((bf, D_MODEL), lambda i, j: (j, 0)),
            pl.BlockSpec((1, D_MODEL), lambda i, j: (0, 0)),
        ],
        out_specs=pl.BlockSpec((bm, D_MODEL), lambda i, j: (i, 0)),
        out_shape=jax.ShapeDtypeStruct((t, D_MODEL), F32),
        scratch_shapes=[pltpu.VMEM((bm, D_MODEL), BF16)],
        compiler_params=pltpu.CompilerParams(
            dimension_semantics=("parallel", "arbitrary"), vmem_limit_bytes=60 * MIB),
        name="swiglu_ffn",
    )(x, g, wg, wu, wd, fg)


def _rope_tables():
    pos = jnp.arange(SEQ, dtype=jnp.int32)

    def angles(p, dim):
        inv = ROPE_THETA ** (-jnp.arange(0, dim, 2, dtype=F32) / dim)
        return p.astype(F32)[:, None] * inv[None, :]

    ang_b = angles(pos, HEAD_DIM)
    ang_a = angles(pos, A_ROPE)
    ang_g = angles(jnp.arange(GRID_W, dtype=jnp.int32), HEAD_DIM // 2)
    cos_g, sin_g = jnp.cos(ang_g), jnp.sin(ang_g)
    cos_r, sin_r = (jnp.repeat(a, GRID_W, axis=0) for a in (cos_g, sin_g))
    cos_k, sin_k = (jnp.tile(a, (SEQ // GRID_W, 1)) for a in (cos_g, sin_g))
    cos_pa, sin_pa = jnp.cos(ang_a), jnp.sin(ang_a)
    cos_pb, sin_pb = jnp.cos(ang_b), jnp.sin(ang_b)
    z = jnp.zeros((SEQ, 32), F32)
    cat = lambda *xs: jnp.concatenate(xs, axis=-1)
    cos_a = cat(cos_pa, z, cos_pa, z)
    sin_a = cat(-sin_pa, z, sin_pa, z)
    cos_b = cat(cos_pb, cos_pb)
    sin_b = cat(-sin_pb, sin_pb)
    cos_c = cat(cos_r, cos_k, cos_r, cos_k)
    sin_c = cat(-sin_r, -sin_k, sin_r, sin_k)
    return cos_a, sin_a, cos_b, sin_b, cos_c, sin_c


def _axial_perm():
    a = jnp.arange(32)
    return jnp.concatenate([a, a + 64, a + 32, a + 96])


def _w_in_layout_kernel(w_ref, o_ref):
    bl = w_ref.shape[1]

    def put(col, rows):
        o_ref[:, col:col + 128] = rows.T.astype(BF16)

    for t in range(OFF_KR // 128):
        put(t * 128, w_ref[t * 128:(t + 1) * 128, :])
    z = jnp.zeros((32, bl), F32)
    put(OFF_KR, jnp.concatenate([w_ref[1024:1056, :], z, w_ref[1056:1088, :], z], axis=0))
    for t in range(3 * B_WIDTH // 128):
        put(OFF_BQ + t * 128, w_ref[1088 + t * 128:1088 + (t + 1) * 128, :])
    for h in range(C_HEADS + C_KV_HEADS):
        r0 = 3392 + h * 128
        put(OFF_CQH + h * 128, jnp.concatenate(
            [w_ref[r0:r0 + 32, :], w_ref[r0 + 64:r0 + 96, :],
             w_ref[r0 + 32:r0 + 64, :], w_ref[r0 + 96:r0 + 128, :]], axis=0))
    for t in range(C_KV_WIDTH // 128):
        put(OFF_CV + t * 128, w_ref[4416 + t * 128:4416 + (t + 1) * 128, :])


def _layout_w_in(w, layer):
    wt = jnp.swapaxes(w, 1, 2)
    bl = 256
    width = wt.shape[1]
    return pl.pallas_call(
        _w_in_layout_kernel,
        grid=(D_MODEL // bl,),
        in_specs=[pl.BlockSpec((None, width, bl), lambda i: (layer, 0, i))],
        out_specs=pl.BlockSpec((bl, PROJ_WIDTH), lambda i: (i, 0)),
        out_shape=jax.ShapeDtypeStruct((D_MODEL, PROJ_WIDTH), BF16),
        compiler_params=pltpu.CompilerParams(
            dimension_semantics=("parallel",), vmem_limit_bytes=40 * MIB),
        name="w_in_layout",
    )(wt)


def _layout_w_uq(w):
    w = w.reshape(A_RANK, A_HEADS, A_NOPE + A_ROPE)
    z = jnp.zeros((A_RANK, A_HEADS, 32), w.dtype)
    out = jnp.concatenate([w[:, :, :128], w[:, :, 128:160], z, w[:, :, 160:192], z], axis=2)
    return out.reshape(A_RANK, A_HEADS * A_DK).astype(BF16)


def _layout_w_ukv(w):
    w = w.reshape(A_RANK, A_HEADS, 2 * HEAD_DIM)
    out = jnp.concatenate([w[:, :, :128].reshape(A_RANK, A_WIDTH),
                           w[:, :, 128:].reshape(A_RANK, A_WIDTH)], axis=1)
    return out.astype(BF16)


def kernel(x, attn_norm, w_in, a_q_norm, a_w_uq, a_kv_norm, a_w_ukv, c_q_norm, c_k_norm,
           out_norm, w_out, ffn_norm, w_gate, w_up, w_down, final_norm):
    bsz, seq, _ = x.shape
    t = bsz * seq
    depth = w_in.shape[0]
    tables = _rope_tables()
    perm = _axial_perm()
    xf = x.reshape(t, D_MODEL)
    for l in range(depth):
        (qa, ka, va, qb1, kb1, vb1, qb4, kb4, vb4, qb16, kb16, vb16, qc, kc, vc) = _proj_prep(
            xf, attn_norm[l][None, :], _layout_w_in(w_in, l), tables,
            a_q_norm[l][None, :], a_kv_norm[l][None, :],
            c_q_norm[l][perm][None, :], c_k_norm[l][perm][None, :],
            _layout_w_uq(a_w_uq[l]), _layout_w_ukv(a_w_ukv[l]))
        r3 = lambda a, n=1: a.reshape(bsz, seq // n, a.shape[1])
        nb = 4
        ya, wd_bf, wo_bf = _attention(
            r3(qa), r3(ka), r3(va), kv_heads=A_HEADS, group=1, dk=A_DK, dv=HEAD_DIM, bq=512,
            name="mla_attn", side_casts=((w_down, l, 176), (w_out, l, 16 * nb)),
            sub=256, n_blk=nb)
        yb = _dilated([r3(qb1), r3(kb1), r3(vb1), r3(qb4, 4), r3(kb4, 4), r3(vb4, 4),
                       r3(qb16, 16), r3(kb16, 16), r3(vb16, 16)])
        yc, wg_bf, wu_bf = _attention(
            r3(qc), r3(kc), r3(vc), kv_heads=C_KV_HEADS, group=C_GROUP, dk=HEAD_DIM, dv=HEAD_DIM,
            bq=256, name="gqa_attn", side_casts=((w_gate, l, 16 * nb), (w_up, l, 16 * nb)),
            sub=1024, n_blk=nb)
        xf = _out_proj(ya.reshape(t, A_WIDTH), yb.reshape(t, B_WIDTH), yc.reshape(t, C_WIDTH),
                       xf, out_norm[l][None, :], wo_bf)
        xf = _ffn(xf, ffn_norm[l][None, :], wg_bf, wu_bf, wd_bf, final_norm[None, :],
                  final_norm=(l == depth - 1))
    return xf.reshape(bsz, seq, D_MODEL)
```

```python
import functools

import jax
import jax.numpy as jnp
from jax import lax
from jax.experimental import pallas as pl
from jax.experimental.pallas import tpu as pltpu

F32 = jnp.float32
BF16 = jnp.bfloat16

D_MODEL = 2048
SEQ = 4096
HEAD_DIM = 128
ROPE_THETA = 10000.0
GRID_W = 64
EPS = 1e-6
NEG = -1e30

A_HEADS = 4
A_RANK = 512
A_NOPE = 128
A_ROPE = 64
A_DK = 256
B_HEADS = 6
B_DILATIONS = (1, 4, 16)
B_HALF = 64
C_HEADS = 6
C_KV_HEADS = 2
C_GROUP = C_HEADS // C_KV_HEADS
A_WIDTH = A_HEADS * HEAD_DIM
B_WIDTH = B_HEADS * HEAD_DIM
C_WIDTH = C_HEADS * HEAD_DIM
C_KV_WIDTH = C_KV_HEADS * HEAD_DIM
D_FF = 5632

OFF_CQ = 0
OFF_CKV = 512
OFF_KR = 1024
OFF_BQ = 1152
OFF_BK = OFF_BQ + B_WIDTH
OFF_BV = OFF_BK + B_WIDTH
OFF_CQH = OFF_BV + B_WIDTH
OFF_CK = OFF_CQH + C_WIDTH
OFF_CV = OFF_CK + C_KV_WIDTH
PROJ_USED = OFF_CV + C_KV_WIDTH
PROJ_WIDTH = PROJ_USED

LOG2E = 1.4426950408889634
SCALE_A = (A_NOPE + A_ROPE) ** -0.5 * LOG2E
SCALE_B = HEAD_DIM ** -0.5 * LOG2E
SCALE_C = HEAD_DIM ** -0.5 * LOG2E

MIB = 1024 * 1024
BF16_SUBLANES = 16


def _rms(x, g):
    ms = jnp.mean(x * x, axis=-1, keepdims=True)
    return x * lax.rsqrt(ms + EPS) * g


def _rope(x, c, s):
    return x * c + pltpu.roll(x, 64, 1) * s


def _resident(shape):
    return pl.BlockSpec(shape, lambda i: (0,) * len(shape), pipeline_mode=pl.Buffered(1))


PREP_BM = 256


def _proj_prep_kernel(x_ref, g_ref, w_ref, cos_a, sin_a, cos_b, sin_b, cos_c, sin_c,
                      gq, gkv, gcq, gck, wuq, wukv, sel_ref,
                      qa, ka, va, qb1, kb1, vb1, qb4, kb4, vb4, qb16, kb16, vb16, qc, kc, vc):
    bm = PREP_BM
    hn = _rms(x_ref[...], g_ref[...]).astype(BF16)

    def project(lo, hi):
        return jnp.dot(hn, w_ref[:, lo:hi], preferred_element_type=F32)

    seg_a = project(0, OFF_BQ)
    seg_bq = project(OFF_BQ, OFF_BK)

    ca, sa = cos_a[...], sin_a[...]
    cq = _rms(seg_a[:, OFF_CQ:OFF_CQ + A_RANK], gq[...]).astype(BF16)
    q = jnp.dot(cq, wuq[...], preferred_element_type=F32)
    for h in range(A_HEADS):
        lo = h * A_DK
        qa[:, lo:lo + 128] = (q[:, lo:lo + 128] * SCALE_A).astype(BF16)
        qa[:, lo + 128:lo + 256] = (_rope(q[:, lo + 128:lo + 256], ca, sa) * SCALE_A).astype(BF16)
    ckv = _rms(seg_a[:, OFF_CKV:OFF_CKV + A_RANK], gkv[...]).astype(BF16)
    kv = jnp.dot(ckv, wukv[...], preferred_element_type=F32)
    kr = _rope(seg_a[:, OFF_KR:OFF_KR + 128], ca, sa).astype(BF16)
    for h in range(A_HEADS):
        lo = h * A_DK
        ka[:, lo:lo + 128] = kv[:, h * 128:(h + 1) * 128].astype(BF16)
        ka[:, lo + 128:lo + 256] = kr
    va[...] = kv[:, A_WIDTH:2 * A_WIDTH].astype(BF16)

    cb, sb = cos_b[...], sin_b[...]
    sel = sel_ref[...]
    ones = jnp.ones((bm, 128), BF16)

    def mixer_b(seg, scale, use_rope, is_v, o1, o4, o16):
        width = 256 if is_v else 128
        tiles = []
        for h in range(B_HEADS):
            x = seg[:, h * 128:(h + 1) * 128]
            if use_rope:
                x = _rope(x, cb, sb)
            if scale is not None:
                x = x * scale
            tiles.append(x.astype(BF16))
            o1[:, h * width:h * width + 128] = tiles[h]
            if is_v:
                o1[:, h * width + 128:(h + 1) * width] = ones
        y = jnp.dot(sel, jnp.concatenate(tiles, axis=1), preferred_element_type=F32)
        for d, od, base in ((4, o4, 0), (16, o16, bm)):
            n = bm // d
            for h in range(B_HEADS):
                for r in range(d):
                    c0 = (h * d + r) * width
                    rows = y[base + r * n:base + (r + 1) * n, h * 128:(h + 1) * 128]
                    od[:, c0:c0 + 128] = rows.astype(BF16)
                    if is_v:
                        od[:, c0 + 128:c0 + 256] = ones[0:n]

    seg_bk = project(OFF_BK, OFF_BV)
    mixer_b(seg_bq, SCALE_B, True, False, qb1, qb4, qb16)
    seg_bv = project(OFF_BV, OFF_CQH)
    mixer_b(seg_bk, None, True, False, kb1, kb4, kb16)
    seg_c = project(OFF_CQH, PROJ_WIDTH)
    mixer_b(seg_bv, None, False, True, vb1, vb4, vb16)

    cc, sc = cos_c[...], sin_c[...]
    for h in range(C_HEADS):
        x = _rms(seg_c[:, h * 128:(h + 1) * 128], gcq[...])
        qc[:, h * 128:(h + 1) * 128] = (_rope(x, cc, sc) * SCALE_C).astype(BF16)
    for h in range(C_KV_HEADS):
        lo = C_WIDTH + h * 128
        x = _rms(seg_c[:, lo:lo + 128], gck[...])
        kc[:, h * 128:(h + 1) * 128] = _rope(x, cc, sc).astype(BF16)
    vc[...] = seg_c[:, C_WIDTH + C_KV_WIDTH:].astype(BF16)


def _proj_prep(x, g, w, tables, gq, gkv, gcq, gck, wuq, wukv):
    t = x.shape[0]
    bm = PREP_BM
    nblk_seq = SEQ // bm
    row = lambda i: (i, 0)
    tab = lambda i: (i % nblk_seq, 0)
    out_widths = [A_HEADS * A_DK, A_HEADS * A_DK, A_WIDTH]
    out_shapes = [jax.ShapeDtypeStruct((t, w_), BF16) for w_ in out_widths]
    out_specs = [pl.BlockSpec((bm, w_), row) for w_ in out_widths]
    for d in B_DILATIONS:
        for w_ in (B_WIDTH, B_WIDTH, 2 * B_WIDTH):
            out_shapes.append(jax.ShapeDtypeStruct((t // d, d * w_), BF16))
            out_specs.append(pl.BlockSpec((bm // d, d * w_), row))
    for w_ in (C_WIDTH, C_KV_WIDTH, C_KV_WIDTH):
        out_shapes.append(jax.ShapeDtypeStruct((t, w_), BF16))
        out_specs.append(pl.BlockSpec((bm, w_), row))
    in_specs = [pl.BlockSpec((bm, D_MODEL), row), _resident((1, D_MODEL)),
                _resident((D_MODEL, PROJ_WIDTH))]
    in_specs += [pl.BlockSpec((bm, 128), tab)] * 6
    in_specs += [_resident((1, A_RANK)), _resident((1, A_RANK)),
                 _resident((1, 128)), _resident((1, 128)),
                 _resident((A_RANK, A_HEADS * A_DK)), _resident((A_RANK, 2 * A_WIDTH)),
                 _resident((2 * bm, bm))]
    return pl.pallas_call(
        _proj_prep_kernel,
        grid=(t // bm,),
        in_specs=in_specs,
        out_specs=out_specs,
        out_shape=out_shapes,
        compiler_params=pltpu.CompilerParams(
            dimension_semantics=("parallel",), vmem_limit_bytes=56 * MIB),
        name="proj_prep",
    )(x, g, w, *tables, gq, gkv, gcq, gck, wuq, wukv, _stream_select(bm))


def _stream_select(bm):
    blocks = []
    for d in B_DILATIONS[1:]:
        out_row = jnp.arange(bm)
        src = (out_row % (bm // d)) * d + out_row // (bm // d)
        blocks.append(jax.nn.one_hot(src, bm, dtype=BF16))
    return jnp.concatenate(blocks, axis=0)


ATTN_KC = 1024
ATTN_SUB = 256


def _attn_kernel(*refs, group, dk, dv, bq, n_blk, n_side, kc, sub):
    q_ref, k_ref, v_ref = refs[0:3]
    side_in = refs[3:3 + n_side]
    o_ref = refs[3 + n_side]
    side_out = refs[4 + n_side:4 + 2 * n_side]
    vt_scr = refs[4 + 2 * n_side]

    for src, dst in zip(side_in, side_out):
        dst[...] = src[...].astype(BF16)

    @pl.when(pl.program_id(2) == 0)
    def _():
        vt_scr[...] = v_ref[0].T

    q_blocks = [jnp.concatenate([q_ref[0, i * bq:(i + 1) * bq, g * dk:(g + 1) * dk]
                                 for g in range(group)], axis=0) for i in range(n_blk)]
    n_chunks = SEQ // kc
    n_sub = kc // sub

    def scores(item, j):
        blk, c = item
        r0 = c * kc + j * sub
        return lax.dot_general(k_ref[0, r0:r0 + sub, :], q_blocks[blk], (((1,), (1,)), ((), ())),
                               preferred_element_type=F32)

    def fold8(x, op):
        return op(x.reshape(x.shape[0] // 8, 8, x.shape[1]), axis=0)

    def col_max(pieces):
        mx = fold8(pieces[0], jnp.max)
        for piece in pieces[1:]:
            mx = jnp.maximum(mx, fold8(piece, jnp.max))
        return jnp.max(mx, axis=0, keepdims=True)

    items = [(blk, c) for blk in range(n_blk) for c in range(n_chunks)]
    s_cur = [scores(items[0], j) for j in range(n_sub)]
    m = l = acc = None
    for idx, (blk, c) in enumerate(items):
        m_c = col_max(s_cur)
        m_new = m_c if c == 0 else jnp.maximum(m, m_c)
        s_next, p_sum, p_bf = [], None, []
        for j in range(n_sub):
            if idx + 1 < len(items):
                s_next.append(scores(items[idx + 1], j))
            p = jnp.exp2(s_cur[j] - m_new)
            p_sum = fold8(p, jnp.sum) if j == 0 else p_sum + fold8(p, jnp.sum)
            p_bf.append(p.astype(BF16))
        l_c = jnp.sum(p_sum, axis=0, keepdims=True)
        pv = jnp.dot(vt_scr[:, c * kc:(c + 1) * kc], jnp.concatenate(p_bf, axis=0),
                     preferred_element_type=F32)
        if c == 0:
            l, acc = l_c, pv
        else:
            alpha = jnp.exp2(m - m_new)
            l = alpha * l + l_c
            acc = alpha * acc + pv
        m = m_new
        s_cur = s_next
        if c == n_chunks - 1:
            out = acc / l
            for g in range(group):
                o_ref[0, blk * bq:(blk + 1) * bq, g * dv:(g + 1) * dv] = (
                    out[:, g * bq:(g + 1) * bq].T)


def _attention(q, k, v, *, kv_heads, group, dk, dv, bq, name, side_casts=(), kc=ATTN_KC,
               sub=ATTN_SUB, n_blk=1):
    b = q.shape[0]
    rows_q = n_blk * bq
    n_q = SEQ // rows_q

    def side_block(rows_total, rows):
        last = rows_total // rows - 1
        return lambda bi, h, qi: jnp.minimum((bi * kv_heads + h) * n_q + qi, last)

    in_specs = [
        pl.BlockSpec((1, rows_q, group * dk), lambda bi, h, qi: (bi, qi, h)),
        pl.BlockSpec((1, SEQ, dk), lambda bi, h, qi: (bi, 0, h)),
        pl.BlockSpec((1, SEQ, dv), lambda bi, h, qi: (bi, 0, h)),
    ]
    out_specs = [pl.BlockSpec((1, rows_q, group * dv), lambda bi, h, qi: (bi, qi, h))]
    out_shapes = [jax.ShapeDtypeStruct((b, SEQ, kv_heads * group * dv), F32)]
    operands = [q, k, v]
    n_steps = b * kv_heads * n_q
    for w, layer in side_casts:
        _, r, c = w.shape
        rows = next(m for m in range(BF16_SUBLANES, r + 1, BF16_SUBLANES)
                    if r % m == 0 and r // m <= n_steps)
        blk = side_block(r, rows)
        in_specs.append(pl.BlockSpec(
            (None, rows, c), lambda bi, h, qi, blk=blk, layer=layer: (layer, blk(bi, h, qi), 0)))
        out_specs.append(pl.BlockSpec((rows, c), lambda bi, h, qi, blk=blk: (blk(bi, h, qi), 0)))
        out_shapes.append(jax.ShapeDtypeStruct((r, c), BF16))
        operands.append(w)
    return pl.pallas_call(
        functools.partial(_attn_kernel, group=group, dk=dk, dv=dv, bq=bq, n_blk=n_blk,
                          n_side=len(side_casts), kc=kc, sub=sub),
        grid=(b, kv_heads, n_q),
        in_specs=in_specs,
        out_specs=out_specs,
        out_shape=out_shapes,
        scratch_shapes=[pltpu.VMEM((dv, SEQ), BF16)],
        compiler_params=pltpu.CompilerParams(
            dimension_semantics=("arbitrary", "arbitrary", "arbitrary"),
            vmem_limit_bytes=48 * MIB),
        name=name,
    )(*operands)


DIL_BQ = 128
DIL_WIN = DIL_BQ + 2 * B_HALF


def _dil_block(q_ref, k_ref, v_ref, bias_ref, stream, q0, length):
    if isinstance(q0, int):
        start = min(max(q0 - B_HALF, 0), length - DIL_WIN)
        case = (q0 - start) // B_HALF
    else:
        start = pl.multiple_of(jnp.clip(q0 - B_HALF, 0, length - DIL_WIN), B_HALF)
        case = lax.shift_right_logical(q0 - start, 6)
    q = q_ref[0, pl.ds(q0, DIL_BQ), stream * 128:(stream + 1) * 128]
    k = k_ref[0, pl.ds(start, DIL_WIN), stream * 128:(stream + 1) * 128]
    vx = v_ref[0, pl.ds(start, DIL_WIN), stream * 256:(stream + 1) * 256]
    s = lax.dot_general(q, k, (((1,), (1,)), ((), ())), preferred_element_type=F32)
    s = s + bias_ref[case]
    m = jnp.max(s, axis=-1, keepdims=True)
    p = jnp.exp2(s - m).astype(BF16)
    ol = jnp.dot(p, vx, preferred_element_type=F32)
    l = ol[:, 128:256]
    return ol[:, 0:128] / l, m + jnp.log(l) * LOG2E


def _dil_kernel(q1, k1, v1, q4, k4, v4, q16, k16, v16, bias_ref, o_ref,
                osm4, lsm4, osm16, lsm16, otok, ltok):
    def body1(n, carry):
        q0 = pl.multiple_of(n * DIL_BQ, DIL_BQ)
        o, lse = _dil_block(q1, k1, v1, bias_ref, 0, q0, SEQ)
        otok[0, pl.ds(q0, DIL_BQ), :] = o
        ltok[0, pl.ds(q0, DIL_BQ), :] = lse
        return carry

    lax.fori_loop(0, SEQ // DIL_BQ, body1, 0, unroll=True)

    def body4(n, carry):
        q0 = pl.multiple_of(n * DIL_BQ, DIL_BQ)
        for r in range(4):
            o, lse = _dil_block(q4, k4, v4, bias_ref, r, q0, SEQ // 4)
            osm4[pl.ds(q0, DIL_BQ), r * 128:(r + 1) * 128] = o
            lsm4[pl.ds(q0, DIL_BQ), r * 128:(r + 1) * 128] = lse
        return carry

    lax.fori_loop(0, SEQ // 4 // DIL_BQ, body4, 0, unroll=True)
    for r in range(4):
        otok[1, pl.ds(r, SEQ // 4, stride=4), :] = osm4[:, r * 128:(r + 1) * 128]
        ltok[1, pl.ds(r, SEQ // 4, stride=4), :] = lsm4[:, r * 128:(r + 1) * 128]

    for n in range(SEQ // 16 // DIL_BQ):
        q0 = n * DIL_BQ
        for r in range(16):
            o, lse = _dil_block(q16, k16, v16, bias_ref, r, q0, SEQ // 16)
            osm16[q0:q0 + DIL_BQ, r * 128:(r + 1) * 128] = o
            lsm16[q0:q0 + DIL_BQ, r * 128:(r + 1) * 128] = lse
    for r in range(16):
        otok[2, pl.ds(r, SEQ // 16, stride=16), :] = osm16[:, r * 128:(r + 1) * 128]
        ltok[2, pl.ds(r, SEQ // 16, stride=16), :] = lsm16[:, r * 128:(r + 1) * 128]

    chunk = 256

    def combine(c, carry):
        r0 = pl.multiple_of(c * chunk, chunk)
        l0 = ltok[0, pl.ds(r0, chunk), :]
        l1 = ltok[1, pl.ds(r0, chunk), :]
        l2 = ltok[2, pl.ds(r0, chunk), :]
        mx = jnp.maximum(jnp.maximum(l0, l1), l2)
        e0 = jnp.exp2(l0 - mx)
        e1 = jnp.exp2(l1 - mx)
        e2 = jnp.exp2(l2 - mx)
        num = (e0 * otok[0, pl.ds(r0, chunk), :] + e1 * otok[1, pl.ds(r0, chunk), :]
               + e2 * otok[2, pl.ds(r0, chunk), :])
        o_ref[0, pl.ds(r0, chunk), :] = num / (e0 + e1 + e2)
        return carry

    lax.fori_loop(0, SEQ // chunk, combine, 0)


def _dil_bias():
    i = jnp.arange(DIL_BQ)[:, None]
    j = jnp.arange(DIL_WIN)[None, :]
    return jnp.stack([jnp.where(jnp.abs(j - i - c * B_HALF) <= B_HALF, 0.0, NEG)
                      for c in range(3)]).astype(F32)


def _dilated(qkv_by_dilation):
    b = qkv_by_dilation[0].shape[0]
    in_specs = []
    for d in B_DILATIONS:
        in_specs += [pl.BlockSpec((1, SEQ // d, d * 128), lambda bi, h: (bi, 0, h))] * 2
        in_specs += [pl.BlockSpec((1, SEQ // d, d * 256), lambda bi, h: (bi, 0, h))]
    in_specs += [pl.BlockSpec((3, DIL_BQ, DIL_WIN), lambda bi, h: (0, 0, 0))]
    qkv_by_dilation = list(qkv_by_dilation) + [_dil_bias()]
    return pl.pallas_call(
        _dil_kernel,
        grid=(b, B_HEADS),
        in_specs=in_specs,
        out_specs=pl.BlockSpec((1, SEQ, 128), lambda bi, h: (bi, 0, h)),
        out_shape=jax.ShapeDtypeStruct((b, SEQ, B_WIDTH), F32),
        scratch_shapes=[
            pltpu.VMEM((SEQ // 4, 4 * 128), F32), pltpu.VMEM((SEQ // 4, 4 * 128), F32),
            pltpu.VMEM((SEQ // 16, 16 * 128), F32), pltpu.VMEM((SEQ // 16, 16 * 128), F32),
            pltpu.VMEM((3, SEQ, 128), F32), pltpu.VMEM((3, SEQ, 128), F32),
        ],
        compiler_params=pltpu.CompilerParams(
            dimension_semantics=("parallel", "parallel"), vmem_limit_bytes=56 * MIB),
        name="dilated_attn",
    )(*qkv_by_dilation)


def _out_proj_kernel(ya, yb, yc, x_ref, g_ref, w_ref, o_ref):
    b0, c0 = A_WIDTH, A_WIDTH + B_WIDTH
    na = _rms(ya[...], g_ref[:, 0:b0]).astype(BF16)
    nb = _rms(yb[...], g_ref[:, b0:c0]).astype(BF16)
    nc = _rms(yc[...], g_ref[:, c0:]).astype(BF16)
    y = (jnp.dot(na, w_ref[0:b0, :], preferred_element_type=F32)
         + jnp.dot(nb, w_ref[b0:c0, :], preferred_element_type=F32)
         + jnp.dot(nc, w_ref[c0:, :], preferred_element_type=F32))
    o_ref[...] = x_ref[...] + y


def _out_proj(ya, yb, yc, x, g, w):
    t = x.shape[0]
    bm = 512
    row = lambda i: (i, 0)
    return pl.pallas_call(
        _out_proj_kernel,
        grid=(t // bm,),
        in_specs=[
            pl.BlockSpec((bm, A_WIDTH), row),
            pl.BlockSpec((bm, B_WIDTH), row),
            pl.BlockSpec((bm, C_WIDTH), row),
            pl.BlockSpec((bm, D_MODEL), row),
            _resident((1, D_MODEL)),
            _resident((D_MODEL, D_MODEL)),
        ],
        out_specs=pl.BlockSpec((bm, D_MODEL), row),
        out_shape=jax.ShapeDtypeStruct((t, D_MODEL), F32),
        compiler_params=pltpu.CompilerParams(
            dimension_semantics=("parallel",), vmem_limit_bytes=48 * MIB),
        name="out_proj",
    )(ya, yb, yc, x, g, w)


FFN_BF = 512


def _ffn_kernel(x_ref, g_ref, wg_ref, wu_ref, wd_ref, fg_ref, o_ref, h_scr, *, final_norm):
    j = pl.program_id(1)

    @pl.when(j == 0)
    def _():
        x = x_ref[...]
        h_scr[...] = _rms(x, g_ref[...]).astype(BF16)
        o_ref[...] = x

    h = h_scr[...]
    gate = jnp.dot(h, wg_ref[...], preferred_element_type=F32)
    up = jnp.dot(h, wu_ref[...], preferred_element_type=F32)
    ff = (gate * jax.nn.sigmoid(gate)) * up
    o_ref[...] += jnp.dot(ff.astype(BF16), wd_ref[...], preferred_element_type=F32)

    if final_norm:
        @pl.when(j == pl.num_programs(1) - 1)
        def _():
            o_ref[...] = _rms(o_ref[...], fg_ref[...])


def _ffn(x, g, wg, wu, wd, fg, *, final_norm):
    t = x.shape[0]
    bm, bf = 1024, FFN_BF
    return pl.pallas_call(
        functools.partial(_ffn_kernel, final_norm=final_norm),
        grid=(t // bm, D_FF // bf),
        in_specs=[
            pl.BlockSpec((bm, D_MODEL), lambda i, j: (i, 0)),
            pl.BlockSpec((1, D_MODEL), lambda i, j: (0, 0)),
            pl.BlockSpec((D_MODEL, bf), lambda i, j: (0, j)),
            pl.BlockSpec((D_MODEL, bf), lambda i, j: (0, j)),
            pl.BlockSpec((bf, D_MODEL), lambda i, j: (j, 0)),
            pl.BlockSpec((1, D_MODEL), lambda i, j: (0, 0)),
        ],
        out_specs=pl.BlockSpec((bm, D_MODEL), lambda i, j: (i, 0)),
        out_shape=jax.ShapeDtypeStruct((t, D_MODEL), F32),
        scratch_shapes=[pltpu.VMEM((bm, D_MODEL), BF16)],
        compiler_params=pltpu.CompilerParams(
            dimension_semantics=("parallel", "arbitrary"), vmem_limit_bytes=60 * MIB),
        name="swiglu_ffn",
    )(x, g, wg, wu, wd, fg)


def _rope_tables():
    pos = jnp.arange(SEQ, dtype=jnp.int32)

    def angles(p, dim):
        inv = ROPE_THETA ** (-jnp.arange(0, dim, 2, dtype=F32) / dim)
        return p.astype(F32)[:, None] * inv[None, :]

    ang_b = angles(pos, HEAD_DIM)
    ang_a = angles(pos, A_ROPE)
    ang_g = angles(jnp.arange(GRID_W, dtype=jnp.int32), HEAD_DIM // 2)
    cos_g, sin_g = jnp.cos(ang_g), jnp.sin(ang_g)
    cos_r, sin_r = (jnp.repeat(a, GRID_W, axis=0) for a in (cos_g, sin_g))
    cos_k, sin_k = (jnp.tile(a, (SEQ // GRID_W, 1)) for a in (cos_g, sin_g))
    cos_pa, sin_pa = jnp.cos(ang_a), jnp.sin(ang_a)
    cos_pb, sin_pb = jnp.cos(ang_b), jnp.sin(ang_b)
    z = jnp.zeros((SEQ, 32), F32)
    cat = lambda *xs: jnp.concatenate(xs, axis=-1)
    cos_a = cat(cos_pa, z, cos_pa, z)
    sin_a = cat(-sin_pa, z, sin_pa, z)
    cos_b = cat(cos_pb, cos_pb)
    sin_b = cat(-sin_pb, sin_pb)
    cos_c = cat(cos_r, cos_k, cos_r, cos_k)
    sin_c = cat(-sin_r, -sin_k, sin_r, sin_k)
    return cos_a, sin_a, cos_b, sin_b, cos_c, sin_c


def _axial_perm():
    a = jnp.arange(32)
    return jnp.concatenate([a, a + 64, a + 32, a + 96])


def _w_in_layout_kernel(w_ref, o_ref):
    bl = w_ref.shape[1]

    def put(col, rows):
        o_ref[:, col:col + 128] = rows.T.astype(BF16)

    for t in range(OFF_KR // 128):
        put(t * 128, w_ref[t * 128:(t + 1) * 128, :])
    z = jnp.zeros((32, bl), F32)
    put(OFF_KR, jnp.concatenate([w_ref[1024:1056, :], z, w_ref[1056:1088, :], z], axis=0))
    for t in range(3 * B_WIDTH // 128):
        put(OFF_BQ + t * 128, w_ref[1088 + t * 128:1088 + (t + 1) * 128, :])
    for h in range(C_HEADS + C_KV_HEADS):
        r0 = 3392 + h * 128
        put(OFF_CQH + h * 128, jnp.concatenate(
            [w_ref[r0:r0 + 32, :], w_ref[r0 + 64:r0 + 96, :],
             w_ref[r0 + 32:r0 + 64, :], w_ref[r0 + 96:r0 + 128, :]], axis=0))
    for t in range(C_KV_WIDTH // 128):
        put(OFF_CV + t * 128, w_ref[4416 + t * 128:4416 + (t + 1) * 128, :])


def _layout_w_in(w, layer):
    wt = jnp.swapaxes(w, 1, 2)
    bl = 256
    width = wt.shape[1]
    return pl.pallas_call(
        _w_in_layout_kernel,
        grid=(D_MODEL // bl,),
        in_specs=[pl.BlockSpec((None, width, bl), lambda i: (layer, 0, i))],
        out_specs=pl.BlockSpec((bl, PROJ_WIDTH), lambda i: (i, 0)),
        out_shape=jax.ShapeDtypeStruct((D_MODEL, PROJ_WIDTH), BF16),
        compiler_params=pltpu.CompilerParams(
            dimension_semantics=("parallel",), vmem_limit_bytes=40 * MIB),
        name="w_in_layout",
    )(wt)


def _layout_w_uq(w):
    w = w.reshape(A_RANK, A_HEADS, A_NOPE + A_ROPE)
    z = jnp.zeros((A_RANK, A_HEADS, 32), w.dtype)
    out = jnp.concatenate([w[:, :, :128], w[:, :, 128:160], z, w[:, :, 160:192], z], axis=2)
    return out.reshape(A_RANK, A_HEADS * A_DK).astype(BF16)


def _layout_w_ukv(w):
    w = w.reshape(A_RANK, A_HEADS, 2 * HEAD_DIM)
    out = jnp.concatenate([w[:, :, :128].reshape(A_RANK, A_WIDTH),
                           w[:, :, 128:].reshape(A_RANK, A_WIDTH)], axis=1)
    return out.astype(BF16)


def kernel(x, attn_norm, w_in, a_q_norm, a_w_uq, a_kv_norm, a_w_ukv, c_q_norm, c_k_norm,
           out_norm, w_out, ffn_norm, w_gate, w_up, w_down, final_norm):
    bsz, seq, _ = x.shape
    t = bsz * seq
    depth = w_in.shape[0]
    tables = _rope_tables()
    perm = _axial_perm()
    xf = x.reshape(t, D_MODEL)
    for l in range(depth):
        (qa, ka, va, qb1, kb1, vb1, qb4, kb4, vb4, qb16, kb16, vb16, qc, kc, vc) = _proj_prep(
            xf, attn_norm[l][None, :], _layout_w_in(w_in, l), tables,
            a_q_norm[l][None, :], a_kv_norm[l][None, :],
            c_q_norm[l][perm][None, :], c_k_norm[l][perm][None, :],
            _layout_w_uq(a_w_uq[l]), _layout_w_ukv(a_w_ukv[l]))
        r3 = lambda a, n=1: a.reshape(bsz, seq // n, a.shape[1])
        nb = 4
        ya, wd_bf, wo_bf = _attention(
            r3(qa), r3(ka), r3(va), kv_heads=A_HEADS, group=1, dk=A_DK, dv=HEAD_DIM, bq=512,
            name="mla_attn", side_casts=((w_down, l), (w_out, l)), sub=256 if l == 0 else 512,
            n_blk=nb)
        yb = _dilated([r3(qb1), r3(kb1), r3(vb1), r3(qb4, 4), r3(kb4, 4), r3(vb4, 4),
                       r3(qb16, 16), r3(kb16, 16), r3(vb16, 16)])
        yc, wg_bf, wu_bf = _attention(
            r3(qc), r3(kc), r3(vc), kv_heads=C_KV_HEADS, group=C_GROUP, dk=HEAD_DIM, dv=HEAD_DIM,
            bq=256, name="gqa_attn", side_casts=((w_gate, l), (w_up, l)),
            sub=1024 if l == 0 else 512, n_blk=nb)
        xf = _out_proj(ya.reshape(t, A_WIDTH), yb.reshape(t, B_WIDTH), yc.reshape(t, C_WIDTH),
                       xf, out_norm[l][None, :], wo_bf)
        xf = _ffn(xf, ffn_norm[l][None, :], wg_bf, wu_bf, wd_bf, final_norm[None, :],
                  final_norm=(l == depth - 1))
    return xf.reshape(bsz, seq, D_MODEL)
```

```python
import functools

import jax
import jax.numpy as jnp
from jax import lax
from jax.experimental import pallas as pl
from jax.experimental.pallas import tpu as pltpu

F32 = jnp.float32
BF16 = jnp.bfloat16

D_MODEL = 2048
SEQ = 4096
HEAD_DIM = 128
ROPE_THETA = 10000.0
GRID_W = 64
EPS = 1e-6
NEG = -1e30

A_HEADS = 4
A_RANK = 512
A_NOPE = 128
A_ROPE = 64
A_DK = 256
B_HEADS = 6
B_DILATIONS = (1, 4, 16)
B_HALF = 64
C_HEADS = 6
C_KV_HEADS = 2
C_GROUP = C_HEADS // C_KV_HEADS
A_WIDTH = A_HEADS * HEAD_DIM
B_WIDTH = B_HEADS * HEAD_DIM
C_WIDTH = C_HEADS * HEAD_DIM
C_KV_WIDTH = C_KV_HEADS * HEAD_DIM
D_FF = 5632

OFF_CQ = 0
OFF_CKV = 512
OFF_KR = 1024
OFF_BQ = 1152
OFF_BK = OFF_BQ + B_WIDTH
OFF_BV = OFF_BK + B_WIDTH
OFF_CQH = OFF_BV + B_WIDTH
OFF_CK = OFF_CQH + C_WIDTH
OFF_CV = OFF_CK + C_KV_WIDTH
PROJ_WIDTH = OFF_CV + C_KV_WIDTH

LOG2E = 1.4426950408889634
SCALE_A = (A_NOPE + A_ROPE) ** -0.5 * LOG2E
SCALE_B = HEAD_DIM ** -0.5 * LOG2E
SCALE_C = HEAD_DIM ** -0.5 * LOG2E

LANES = 128
BF16_SUBLANES = 16
MIB = 1024 * 1024
V7X_VMEM_BYTES = 64 * MIB

PREP_BM = 256
OUT_BM = 512
MLA_BQ = 512
GQA_BQ = 256
FFN_BM = 1024
FFN_BF = 512
VMEM_LIMIT = {
    "proj_prep": 56 * MIB,
    "attention": 48 * MIB,
    "dilated": 56 * MIB,
    "out_proj": 48 * MIB,
    "ffn": 60 * MIB,
    "w_in_layout": 40 * MIB,
}
assert max(VMEM_LIMIT.values()) < V7X_VMEM_BYTES


def _rms(x, g):
    ms = jnp.mean(x * x, axis=-1, keepdims=True)
    return x * lax.rsqrt(ms + EPS) * g


def _rope(x, c, s):
    return x * c + pltpu.roll(x, LANES // 2, 1) * s


def _resident(shape):
    return pl.BlockSpec(shape, lambda i: (0,) * len(shape), pipeline_mode=pl.Buffered(1))


def _proj_prep_kernel(x_ref, g_ref, w_ref, cos_a, sin_a, cos_b, sin_b, cos_c, sin_c,
                      gq, gkv, gcq, gck, wuq, wukv, sel_ref,
                      qa, ka, va, qb1, kb1, vb1, qb4, kb4, vb4, qb16, kb16, vb16, qc, kc, vc):
    bm = PREP_BM
    hn = _rms(x_ref[...], g_ref[...]).astype(BF16)

    def project(lo, hi):
        return jnp.dot(hn, w_ref[:, lo:hi], preferred_element_type=F32)

    seg_a = project(0, OFF_BQ)
    seg_bq = project(OFF_BQ, OFF_BK)

    ca, sa = cos_a[...], sin_a[...]
    cq = _rms(seg_a[:, OFF_CQ:OFF_CQ + A_RANK], gq[...]).astype(BF16)
    q = jnp.dot(cq, wuq[...], preferred_element_type=F32)
    for h in range(A_HEADS):
        lo = h * A_DK
        qa[:, lo:lo + 128] = (q[:, lo:lo + 128] * SCALE_A).astype(BF16)
        qa[:, lo + 128:lo + 256] = (_rope(q[:, lo + 128:lo + 256], ca, sa) * SCALE_A).astype(BF16)
    ckv = _rms(seg_a[:, OFF_CKV:OFF_CKV + A_RANK], gkv[...]).astype(BF16)
    kv = jnp.dot(ckv, wukv[...], preferred_element_type=F32)
    kr = _rope(seg_a[:, OFF_KR:OFF_KR + 128], ca, sa).astype(BF16)
    for h in range(A_HEADS):
        lo = h * A_DK
        ka[:, lo:lo + 128] = kv[:, h * 128:(h + 1) * 128].astype(BF16)
        ka[:, lo + 128:lo + 256] = kr
    va[...] = kv[:, A_WIDTH:2 * A_WIDTH].astype(BF16)

    cb, sb = cos_b[...], sin_b[...]
    sel = sel_ref[...]
    ones = jnp.ones((bm, 128), BF16)

    def mixer_b(seg, scale, use_rope, is_v, o1, o4, o16):
        width = 256 if is_v else 128
        tiles = []
        for h in range(B_HEADS):
            x = seg[:, h * 128:(h + 1) * 128]
            if use_rope:
                x = _rope(x, cb, sb)
            if scale is not None:
                x = x * scale
            tiles.append(x.astype(BF16))
            o1[:, h * width:h * width + 128] = tiles[h]
            if is_v:
                o1[:, h * width + 128:(h + 1) * width] = ones
        y = jnp.dot(sel, jnp.concatenate(tiles, axis=1), preferred_element_type=F32)
        for d, od, base in ((4, o4, 0), (16, o16, bm)):
            n = bm // d
            for h in range(B_HEADS):
                for r in range(d):
                    c0 = (h * d + r) * width
                    rows = y[base + r * n:base + (r + 1) * n, h * 128:(h + 1) * 128]
                    od[:, c0:c0 + 128] = rows.astype(BF16)
                    if is_v:
                        od[:, c0 + 128:c0 + 256] = ones[0:n]

    seg_bk = project(OFF_BK, OFF_BV)
    mixer_b(seg_bq, SCALE_B, True, False, qb1, qb4, qb16)
    seg_bv = project(OFF_BV, OFF_CQH)
    mixer_b(seg_bk, None, True, False, kb1, kb4, kb16)
    seg_c = project(OFF_CQH, PROJ_WIDTH)
    mixer_b(seg_bv, None, False, True, vb1, vb4, vb16)

    cc, sc = cos_c[...], sin_c[...]
    for h in range(C_HEADS):
        x = _rms(seg_c[:, h * 128:(h + 1) * 128], gcq[...])
        qc[:, h * 128:(h + 1) * 128] = (_rope(x, cc, sc) * SCALE_C).astype(BF16)
    for h in range(C_KV_HEADS):
        lo = C_WIDTH + h * 128
        x = _rms(seg_c[:, lo:lo + 128], gck[...])
        kc[:, h * 128:(h + 1) * 128] = _rope(x, cc, sc).astype(BF16)
    vc[...] = seg_c[:, C_WIDTH + C_KV_WIDTH:].astype(BF16)


def _proj_prep(x, g, w, tables, gq, gkv, gcq, gck, wuq, wukv):
    t = x.shape[0]
    bm = PREP_BM
    nblk_seq = SEQ // bm
    row = lambda i: (i, 0)
    tab = lambda i: (i % nblk_seq, 0)
    out_widths = [A_HEADS * A_DK, A_HEADS * A_DK, A_WIDTH]
    out_shapes = [jax.ShapeDtypeStruct((t, w_), BF16) for w_ in out_widths]
    out_specs = [pl.BlockSpec((bm, w_), row) for w_ in out_widths]
    for d in B_DILATIONS:
        for w_ in (B_WIDTH, B_WIDTH, 2 * B_WIDTH):
            out_shapes.append(jax.ShapeDtypeStruct((t // d, d * w_), BF16))
            out_specs.append(pl.BlockSpec((bm // d, d * w_), row))
    for w_ in (C_WIDTH, C_KV_WIDTH, C_KV_WIDTH):
        out_shapes.append(jax.ShapeDtypeStruct((t, w_), BF16))
        out_specs.append(pl.BlockSpec((bm, w_), row))
    in_specs = [pl.BlockSpec((bm, D_MODEL), row), _resident((1, D_MODEL)),
                _resident((D_MODEL, PROJ_WIDTH))]
    in_specs += [pl.BlockSpec((bm, 128), tab)] * 6
    in_specs += [_resident((1, A_RANK)), _resident((1, A_RANK)),
                 _resident((1, 128)), _resident((1, 128)),
                 _resident((A_RANK, A_HEADS * A_DK)), _resident((A_RANK, 2 * A_WIDTH)),
                 _resident((2 * bm, bm))]
    return pl.pallas_call(
        _proj_prep_kernel,
        grid=(t // bm,),
        in_specs=in_specs,
        out_specs=out_specs,
        out_shape=out_shapes,
        compiler_params=pltpu.CompilerParams(
            dimension_semantics=("parallel",), vmem_limit_bytes=VMEM_LIMIT["proj_prep"]),
        name="proj_prep",
    )(x, g, w, *tables, gq, gkv, gcq, gck, wuq, wukv, _stream_select(bm))


def _stream_select(bm):
    blocks = []
    for d in B_DILATIONS[1:]:
        out_row = jnp.arange(bm)
        src = (out_row % (bm // d)) * d + out_row // (bm // d)
        blocks.append(jax.nn.one_hot(src, bm, dtype=BF16))
    return jnp.concatenate(blocks, axis=0)


ATTN_KC = 1024
ATTN_SUB = 512
ATTN_BLOCKS = 4


def _attn_kernel(*refs, group, dk, dv, bq, n_side):
    n_blk, kc, sub = ATTN_BLOCKS, ATTN_KC, ATTN_SUB
    q_ref, k_ref, v_ref = refs[0:3]
    side_in = refs[3:3 + n_side]
    o_ref = refs[3 + n_side]
    side_out = refs[4 + n_side:4 + 2 * n_side]
    vt_scr = refs[4 + 2 * n_side]

    for src, dst in zip(side_in, side_out):
        dst[...] = src[...].astype(BF16)

    @pl.when(pl.program_id(2) == 0)
    def _():
        vt_scr[...] = v_ref[0].T

    q_blocks = [jnp.concatenate([q_ref[0, i * bq:(i + 1) * bq, g * dk:(g + 1) * dk]
                                 for g in range(group)], axis=0) for i in range(n_blk)]
    n_chunks = SEQ // kc
    n_sub = kc // sub

    def scores(item, j):
        blk, c = item
        r0 = c * kc + j * sub
        return lax.dot_general(k_ref[0, r0:r0 + sub, :], q_blocks[blk], (((1,), (1,)), ((), ())),
                               preferred_element_type=F32)

    def fold8(x, op):
        return op(x.reshape(x.shape[0] // 8, 8, x.shape[1]), axis=0)

    def col_max(pieces):
        mx = fold8(pieces[0], jnp.max)
        for piece in pieces[1:]:
            mx = jnp.maximum(mx, fold8(piece, jnp.max))
        return jnp.max(mx, axis=0, keepdims=True)

    items = [(blk, c) for blk in range(n_blk) for c in range(n_chunks)]
    s_cur = [scores(items[0], j) for j in range(n_sub)]
    m = l = acc = None
    for idx, (blk, c) in enumerate(items):
        m_c = col_max(s_cur)
        m_new = m_c if c == 0 else jnp.maximum(m, m_c)
        s_next, p_sum, p_bf = [], None, []
        for j in range(n_sub):
            if idx + 1 < len(items):
                s_next.append(scores(items[idx + 1], j))
            p = jnp.exp2(s_cur[j] - m_new)
            p_sum = fold8(p, jnp.sum) if j == 0 else p_sum + fold8(p, jnp.sum)
            p_bf.append(p.astype(BF16))
        l_c = jnp.sum(p_sum, axis=0, keepdims=True)
        pv = jnp.dot(vt_scr[:, c * kc:(c + 1) * kc], jnp.concatenate(p_bf, axis=0),
                     preferred_element_type=F32)
        if c == 0:
            l, acc = l_c, pv
        else:
            alpha = jnp.exp2(m - m_new)
            l = alpha * l + l_c
            acc = alpha * acc + pv
        m = m_new
        s_cur = s_next
        if c == n_chunks - 1:
            out = acc / l
            for g in range(group):
                o_ref[0, blk * bq:(blk + 1) * bq, g * dv:(g + 1) * dv] = (
                    out[:, g * bq:(g + 1) * bq].T)


def _attention(q, k, v, *, kv_heads, group, dk, dv, bq, name, side_casts=()):
    b = q.shape[0]
    rows_q = ATTN_BLOCKS * bq
    n_q = SEQ // rows_q

    def side_block(rows_total, rows):
        last = rows_total // rows - 1
        return lambda bi, h, qi: jnp.minimum((bi * kv_heads + h) * n_q + qi, last)

    in_specs = [
        pl.BlockSpec((1, rows_q, group * dk), lambda bi, h, qi: (bi, qi, h)),
        pl.BlockSpec((1, SEQ, dk), lambda bi, h, qi: (bi, 0, h)),
        pl.BlockSpec((1, SEQ, dv), lambda bi, h, qi: (bi, 0, h)),
    ]
    out_specs = [pl.BlockSpec((1, rows_q, group * dv), lambda bi, h, qi: (bi, qi, h))]
    out_shapes = [jax.ShapeDtypeStruct((b, SEQ, kv_heads * group * dv), F32)]
    operands = [q, k, v]
    n_steps = b * kv_heads * n_q
    for w, layer in side_casts:
        _, r, c = w.shape
        rows = next(m for m in range(BF16_SUBLANES, r + 1, BF16_SUBLANES)
                    if r % m == 0 and r // m <= n_steps)
        blk = side_block(r, rows)
        in_specs.append(pl.BlockSpec(
            (None, rows, c), lambda bi, h, qi, blk=blk, layer=layer: (layer, blk(bi, h, qi), 0)))
        out_specs.append(pl.BlockSpec((rows, c), lambda bi, h, qi, blk=blk: (blk(bi, h, qi), 0)))
        out_shapes.append(jax.ShapeDtypeStruct((r, c), BF16))
        operands.append(w)
    return pl.pallas_call(
        functools.partial(_attn_kernel, group=group, dk=dk, dv=dv, bq=bq, n_side=len(side_casts)),
        grid=(b, kv_heads, n_q),
        in_specs=in_specs,
        out_specs=out_specs,
        out_shape=out_shapes,
        scratch_shapes=[pltpu.VMEM((dv, SEQ), BF16)],
        compiler_params=pltpu.CompilerParams(
            dimension_semantics=("arbitrary", "arbitrary", "arbitrary"),
            vmem_limit_bytes=VMEM_LIMIT["attention"]),
        name=name,
    )(*operands)


DIL_BQ = 128
DIL_WIN = DIL_BQ + 2 * B_HALF


def _dil_block(q_ref, k_ref, v_ref, bias_ref, stream, q0, length):
    if isinstance(q0, int):
        start = min(max(q0 - B_HALF, 0), length - DIL_WIN)
        case = (q0 - start) // B_HALF
    else:
        start = pl.multiple_of(jnp.clip(q0 - B_HALF, 0, length - DIL_WIN), B_HALF)
        case = lax.shift_right_logical(q0 - start, 6)
    q = q_ref[0, pl.ds(q0, DIL_BQ), stream * 128:(stream + 1) * 128]
    k = k_ref[0, pl.ds(start, DIL_WIN), stream * 128:(stream + 1) * 128]
    vx = v_ref[0, pl.ds(start, DIL_WIN), stream * 256:(stream + 1) * 256]
    s = lax.dot_general(q, k, (((1,), (1,)), ((), ())), preferred_element_type=F32)
    s = s + bias_ref[case]
    m = jnp.max(s, axis=-1, keepdims=True)
    p = jnp.exp2(s - m).astype(BF16)
    ol = jnp.dot(p, vx, preferred_element_type=F32)
    l = ol[:, 128:256]
    return ol[:, 0:128] / l, m + jnp.log(l) * LOG2E


def _dil_kernel(q1, k1, v1, q4, k4, v4, q16, k16, v16, bias_ref, o_ref,
                osm4, lsm4, osm16, lsm16, otok, ltok):
    def body1(n, carry):
        q0 = pl.multiple_of(n * DIL_BQ, DIL_BQ)
        o, lse = _dil_block(q1, k1, v1, bias_ref, 0, q0, SEQ)
        otok[0, pl.ds(q0, DIL_BQ), :] = o
        ltok[0, pl.ds(q0, DIL_BQ), :] = lse
        return carry

    lax.fori_loop(0, SEQ // DIL_BQ, body1, 0, unroll=True)

    def body4(n, carry):
        q0 = pl.multiple_of(n * DIL_BQ, DIL_BQ)
        for r in range(4):
            o, lse = _dil_block(q4, k4, v4, bias_ref, r, q0, SEQ // 4)
            osm4[pl.ds(q0, DIL_BQ), r * 128:(r + 1) * 128] = o
            lsm4[pl.ds(q0, DIL_BQ), r * 128:(r + 1) * 128] = lse
        return carry

    lax.fori_loop(0, SEQ // 4 // DIL_BQ, body4, 0, unroll=True)
    for r in range(4):
        otok[1, pl.ds(r, SEQ // 4, stride=4), :] = osm4[:, r * 128:(r + 1) * 128]
        ltok[1, pl.ds(r, SEQ // 4, stride=4), :] = lsm4[:, r * 128:(r + 1) * 128]

    for n in range(SEQ // 16 // DIL_BQ):
        q0 = n * DIL_BQ
        for r in range(16):
            o, lse = _dil_block(q16, k16, v16, bias_ref, r, q0, SEQ // 16)
            osm16[q0:q0 + DIL_BQ, r * 128:(r + 1) * 128] = o
            lsm16[q0:q0 + DIL_BQ, r * 128:(r + 1) * 128] = lse
    for r in range(16):
        otok[2, pl.ds(r, SEQ // 16, stride=16), :] = osm16[:, r * 128:(r + 1) * 128]
        ltok[2, pl.ds(r, SEQ // 16, stride=16), :] = lsm16[:, r * 128:(r + 1) * 128]

    chunk = 256

    def combine(c, carry):
        r0 = pl.multiple_of(c * chunk, chunk)
        l0 = ltok[0, pl.ds(r0, chunk), :]
        l1 = ltok[1, pl.ds(r0, chunk), :]
        l2 = ltok[2, pl.ds(r0, chunk), :]
        mx = jnp.maximum(jnp.maximum(l0, l1), l2)
        e0 = jnp.exp2(l0 - mx)
        e1 = jnp.exp2(l1 - mx)
        e2 = jnp.exp2(l2 - mx)
        num = (e0 * otok[0, pl.ds(r0, chunk), :] + e1 * otok[1, pl.ds(r0, chunk), :]
               + e2 * otok[2, pl.ds(r0, chunk), :])
        o_ref[0, pl.ds(r0, chunk), :] = num / (e0 + e1 + e2)
        return carry

    lax.fori_loop(0, SEQ // chunk, combine, 0)


def _dil_bias():
    i = jnp.arange(DIL_BQ)[:, None]
    j = jnp.arange(DIL_WIN)[None, :]
    return jnp.stack([jnp.where(jnp.abs(j - i - c * B_HALF) <= B_HALF, 0.0, NEG)
                      for c in range(3)]).astype(F32)


def _dilated(qkv_by_dilation):
    b = qkv_by_dilation[0].shape[0]
    in_specs = []
    for d in B_DILATIONS:
        in_specs += [pl.BlockSpec((1, SEQ // d, d * 128), lambda bi, h: (bi, 0, h))] * 2
        in_specs += [pl.BlockSpec((1, SEQ // d, d * 256), lambda bi, h: (bi, 0, h))]
    in_specs += [pl.BlockSpec((3, DIL_BQ, DIL_WIN), lambda bi, h: (0, 0, 0))]
    qkv_by_dilation = list(qkv_by_dilation) + [_dil_bias()]
    return pl.pallas_call(
        _dil_kernel,
        grid=(b, B_HEADS),
        in_specs=in_specs,
        out_specs=pl.BlockSpec((1, SEQ, 128), lambda bi, h: (bi, 0, h)),
        out_shape=jax.ShapeDtypeStruct((b, SEQ, B_WIDTH), F32),
        scratch_shapes=[
            pltpu.VMEM((SEQ // 4, 4 * 128), F32), pltpu.VMEM((SEQ // 4, 4 * 128), F32),
            pltpu.VMEM((SEQ // 16, 16 * 128), F32), pltpu.VMEM((SEQ // 16, 16 * 128), F32),
            pltpu.VMEM((3, SEQ, 128), F32), pltpu.VMEM((3, SEQ, 128), F32),
        ],
        compiler_params=pltpu.CompilerParams(
            dimension_semantics=("parallel", "parallel"), vmem_limit_bytes=VMEM_LIMIT["dilated"]),
        name="dilated_attn",
    )(*qkv_by_dilation)


def _out_proj_kernel(ya, yb, yc, x_ref, g_ref, w_ref, o_ref):
    b0, c0 = A_WIDTH, A_WIDTH + B_WIDTH
    na = _rms(ya[...], g_ref[:, 0:b0]).astype(BF16)
    nb = _rms(yb[...], g_ref[:, b0:c0]).astype(BF16)
    nc = _rms(yc[...], g_ref[:, c0:]).astype(BF16)
    y = (jnp.dot(na, w_ref[0:b0, :], preferred_element_type=F32)
         + jnp.dot(nb, w_ref[b0:c0, :], preferred_element_type=F32)
         + jnp.dot(nc, w_ref[c0:, :], preferred_element_type=F32))
    o_ref[...] = x_ref[...] + y


def _out_proj(ya, yb, yc, x, g, w):
    t = x.shape[0]
    bm = OUT_BM
    row = lambda i: (i, 0)
    return pl.pallas_call(
        _out_proj_kernel,
        grid=(t // bm,),
        in_specs=[
            pl.BlockSpec((bm, A_WIDTH), row),
            pl.BlockSpec((bm, B_WIDTH), row),
            pl.BlockSpec((bm, C_WIDTH), row),
            pl.BlockSpec((bm, D_MODEL), row),
            _resident((1, D_MODEL)),
            _resident((D_MODEL, D_MODEL)),
        ],
        out_specs=pl.BlockSpec((bm, D_MODEL), row),
        out_shape=jax.ShapeDtypeStruct((t, D_MODEL), F32),
        compiler_params=pltpu.CompilerParams(
            dimension_semantics=("parallel",), vmem_limit_bytes=VMEM_LIMIT["out_proj"]),
        name="out_proj",
    )(ya, yb, yc, x, g, w)


def _ffn_kernel(x_ref, g_ref, wg_ref, wu_ref, wd_ref, fg_ref, o_ref, h_scr, *, final_norm):
    j = pl.program_id(1)

    @pl.when(j == 0)
    def _():
        x = x_ref[...]
        h_scr[...] = _rms(x, g_ref[...]).astype(BF16)
        o_ref[...] = x

    h = h_scr[...]
    gate = jnp.dot(h, wg_ref[...], preferred_element_type=F32)
    up = jnp.dot(h, wu_ref[...], preferred_element_type=F32)
    ff = (gate * jax.nn.sigmoid(gate)) * up
    o_ref[...] += jnp.dot(ff.astype(BF16), wd_ref[...], preferred_element_type=F32)

    if final_norm:
        @pl.when(j == pl.num_programs(1) - 1)
        def _():
            o_ref[...] = _rms(o_ref[...], fg_ref[...])


def _ffn(x, g, wg, wu, wd, fg, *, final_norm):
    t = x.shape[0]
    bm, bf = FFN_BM, FFN_BF
    return pl.pallas_call(
        functools.partial(_ffn_kernel, final_norm=final_norm),
        grid=(t // bm, D_FF // bf),
        in_specs=[
            pl.BlockSpec((bm, D_MODEL), lambda i, j: (i, 0)),
            pl.BlockSpec((1, D_MODEL), lambda i, j: (0, 0)),
            pl.BlockSpec((D_MODEL, bf), lambda i, j: (0, j)),
            pl.BlockSpec((D_MODEL, bf), lambda i, j: (0, j)),
            pl.BlockSpec((bf, D_MODEL), lambda i, j: (j, 0)),
            pl.BlockSpec((1, D_MODEL), lambda i, j: (0, 0)),
        ],
        out_specs=pl.BlockSpec((bm, D_MODEL), lambda i, j: (i, 0)),
        out_shape=jax.ShapeDtypeStruct((t, D_MODEL), F32),
        scratch_shapes=[pltpu.VMEM((bm, D_MODEL), BF16)],
        compiler_params=pltpu.CompilerParams(
            dimension_semantics=("parallel", "arbitrary"), vmem_limit_bytes=VMEM_LIMIT["ffn"]),
        name="swiglu_ffn",
    )(x, g, wg, wu, wd, fg)


def _rope_tables():
    pos = jnp.arange(SEQ, dtype=jnp.int32)

    def angles(p, dim):
        inv = ROPE_THETA ** (-jnp.arange(0, dim, 2, dtype=F32) / dim)
        return p.astype(F32)[:, None] * inv[None, :]

    ang_b = angles(pos, HEAD_DIM)
    ang_a = angles(pos, A_ROPE)
    ang_g = angles(jnp.arange(GRID_W, dtype=jnp.int32), HEAD_DIM // 2)
    cos_g, sin_g = jnp.cos(ang_g), jnp.sin(ang_g)
    cos_r, sin_r = (jnp.repeat(a, GRID_W, axis=0) for a in (cos_g, sin_g))
    cos_k, sin_k = (jnp.tile(a, (SEQ // GRID_W, 1)) for a in (cos_g, sin_g))
    cos_pa, sin_pa = jnp.cos(ang_a), jnp.sin(ang_a)
    cos_pb, sin_pb = jnp.cos(ang_b), jnp.sin(ang_b)
    z = jnp.zeros((SEQ, 32), F32)
    cat = lambda *xs: jnp.concatenate(xs, axis=-1)
    cos_a = cat(cos_pa, z, cos_pa, z)
    sin_a = cat(-sin_pa, z, sin_pa, z)
    cos_b = cat(cos_pb, cos_pb)
    sin_b = cat(-sin_pb, sin_pb)
    cos_c = cat(cos_r, cos_k, cos_r, cos_k)
    sin_c = cat(-sin_r, -sin_k, sin_r, sin_k)
    return cos_a, sin_a, cos_b, sin_b, cos_c, sin_c


def _axial_perm():
    a = jnp.arange(32)
    return jnp.concatenate([a, a + 64, a + 32, a + 96])


def _w_in_layout_kernel(w_ref, o_ref):
    bl = w_ref.shape[1]

    def put(col, rows):
        o_ref[:, col:col + 128] = rows.T.astype(BF16)

    for t in range(OFF_KR // 128):
        put(t * 128, w_ref[t * 128:(t + 1) * 128, :])
    z = jnp.zeros((32, bl), F32)
    put(OFF_KR, jnp.concatenate([w_ref[1024:1056, :], z, w_ref[1056:1088, :], z], axis=0))
    for t in range(3 * B_WIDTH // 128):
        put(OFF_BQ + t * 128, w_ref[1088 + t * 128:1088 + (t + 1) * 128, :])
    for h in range(C_HEADS + C_KV_HEADS):
        r0 = 3392 + h * 128
        put(OFF_CQH + h * 128, jnp.concatenate(
            [w_ref[r0:r0 + 32, :], w_ref[r0 + 64:r0 + 96, :],
             w_ref[r0 + 32:r0 + 64, :], w_ref[r0 + 96:r0 + 128, :]], axis=0))
    for t in range(C_KV_WIDTH // 128):
        put(OFF_CV + t * 128, w_ref[4416 + t * 128:4416 + (t + 1) * 128, :])


def _layout_w_in(w, layer):
    wt = jnp.swapaxes(w, 1, 2)
    bl = 256
    width = wt.shape[1]
    return pl.pallas_call(
        _w_in_layout_kernel,
        grid=(D_MODEL // bl,),
        in_specs=[pl.BlockSpec((None, width, bl), lambda i: (layer, 0, i))],
        out_specs=pl.BlockSpec((bl, PROJ_WIDTH), lambda i: (i, 0)),
        out_shape=jax.ShapeDtypeStruct((D_MODEL, PROJ_WIDTH), BF16),
        compiler_params=pltpu.CompilerParams(
            dimension_semantics=("parallel",), vmem_limit_bytes=VMEM_LIMIT["w_in_layout"]),
        name="w_in_layout",
    )(wt)


def _layout_w_uq(w):
    w = w.reshape(A_RANK, A_HEADS, A_NOPE + A_ROPE)
    z = jnp.zeros((A_RANK, A_HEADS, 32), w.dtype)
    out = jnp.concatenate([w[:, :, :128], w[:, :, 128:160], z, w[:, :, 160:192], z], axis=2)
    return out.reshape(A_RANK, A_HEADS * A_DK).astype(BF16)


def _layout_w_ukv(w):
    w = w.reshape(A_RANK, A_HEADS, 2 * HEAD_DIM)
    out = jnp.concatenate([w[:, :, :128].reshape(A_RANK, A_WIDTH),
                           w[:, :, 128:].reshape(A_RANK, A_WIDTH)], axis=1)
    return out.astype(BF16)


def kernel(x, attn_norm, w_in, a_q_norm, a_w_uq, a_kv_norm, a_w_ukv, c_q_norm, c_k_norm,
           out_norm, w_out, ffn_norm, w_gate, w_up, w_down, final_norm):
    bsz, seq, _ = x.shape
    t = bsz * seq
    depth = w_in.shape[0]
    tables = _rope_tables()
    perm = _axial_perm()
    xf = x.reshape(t, D_MODEL)
    for l in range(depth):
        (qa, ka, va, qb1, kb1, vb1, qb4, kb4, vb4, qb16, kb16, vb16, qc, kc, vc) = _proj_prep(
            xf, attn_norm[l][None, :], _layout_w_in(w_in, l), tables,
            a_q_norm[l][None, :], a_kv_norm[l][None, :],
            c_q_norm[l][perm][None, :], c_k_norm[l][perm][None, :],
            _layout_w_uq(a_w_uq[l]), _layout_w_ukv(a_w_ukv[l]))
        r3 = lambda a, n=1: a.reshape(bsz, seq // n, a.shape[1])
        ya, wd_bf, wo_bf = _attention(
            r3(qa), r3(ka), r3(va), kv_heads=A_HEADS, group=1, dk=A_DK, dv=HEAD_DIM, bq=MLA_BQ,
            name="mla_attn", side_casts=((w_down, l), (w_out, l)))
        yb = _dilated([r3(qb1), r3(kb1), r3(vb1), r3(qb4, 4), r3(kb4, 4), r3(vb4, 4),
                       r3(qb16, 16), r3(kb16, 16), r3(vb16, 16)])
        yc, wg_bf, wu_bf = _attention(
            r3(qc), r3(kc), r3(vc), kv_heads=C_KV_HEADS, group=C_GROUP, dk=HEAD_DIM, dv=HEAD_DIM,
            bq=GQA_BQ, name="gqa_attn", side_casts=((w_gate, l), (w_up, l)))
        xf = _out_proj(ya.reshape(t, A_WIDTH), yb.reshape(t, B_WIDTH), yc.reshape(t, C_WIDTH),
                       xf, out_norm[l][None, :], wo_bf)
        xf = _ffn(xf, ffn_norm[l][None, :], wg_bf, wu_bf, wd_bf, final_norm[None, :],
                  final_norm=(l == depth - 1))
    return xf.reshape(bsz, seq, D_MODEL)
```

```python
import functools

import jax
import jax.numpy as jnp
from jax import lax
from jax.experimental import pallas as pl
from jax.experimental.pallas import tpu as pltpu

F32 = jnp.float32
BF16 = jnp.bfloat16

D_MODEL = 2048
SEQ = 4096
HEAD_DIM = 128
ROPE_THETA = 10000.0
GRID_W = 64
EPS = 1e-6
NEG = -1e30

A_HEADS = 4
A_RANK = 512
A_NOPE = 128
A_ROPE = 64
A_DK = 256
B_HEADS = 6
B_DILATIONS = (1, 4, 16)
B_HALF = 64
C_HEADS = 6
C_KV_HEADS = 2
C_GROUP = C_HEADS // C_KV_HEADS
A_WIDTH = A_HEADS * HEAD_DIM
B_WIDTH = B_HEADS * HEAD_DIM
C_WIDTH = C_HEADS * HEAD_DIM
C_KV_WIDTH = C_KV_HEADS * HEAD_DIM
D_FF = 5632

OFF_CQ = 0
OFF_CKV = 512
OFF_KR = 1024
OFF_BQ = 1152
OFF_BK = OFF_BQ + B_WIDTH
OFF_BV = OFF_BK + B_WIDTH
OFF_CQH = OFF_BV + B_WIDTH
OFF_CK = OFF_CQH + C_WIDTH
OFF_CV = OFF_CK + C_KV_WIDTH
PROJ_WIDTH = OFF_CV + C_KV_WIDTH

LOG2E = 1.4426950408889634
SCALE_A = (A_NOPE + A_ROPE) ** -0.5 * LOG2E
SCALE_B = HEAD_DIM ** -0.5 * LOG2E
SCALE_C = HEAD_DIM ** -0.5 * LOG2E

LANES = 128
BF16_SUBLANES = 16
MIB = 1024 * 1024
V7X_VMEM_BYTES = 64 * MIB

PREP_BM = 256
OUT_BM = 512
MLA_BQ = 512
GQA_BQ = 256
FFN_BM = 1024
FFN_BF = 512
VMEM_LIMIT = {
    "proj_prep": 56 * MIB,
    "attention": 48 * MIB,
    "dilated": 56 * MIB,
    "out_proj": 48 * MIB,
    "ffn": 60 * MIB,
    "w_in_layout": 40 * MIB,
}
assert max(VMEM_LIMIT.values()) < V7X_VMEM_BYTES


def _rms(x, g):
    ms = jnp.mean(x * x, axis=-1, keepdims=True)
    return x * lax.rsqrt(ms + EPS) * g


def _rope(x, c, s):
    return x * c + pltpu.roll(x, LANES // 2, 1) * s


def _resident(shape):
    return pl.BlockSpec(shape, lambda i: (0,) * len(shape), pipeline_mode=pl.Buffered(1))


def _proj_prep_kernel(x_ref, g_ref, w_ref, cos_a, sin_a, cos_b, sin_b, cos_c, sin_c,
                      gq, gkv, gcq, gck, wuq, wukv, sel_ref,
                      qa, ka, va, qb1, kb1, vb1, qb4, kb4, vb4, qb16, kb16, vb16, qc, kc, vc, scr):
    bm = PREP_BM
    hn = _rms(x_ref[...], g_ref[...]).astype(BF16)

    def project(lo, hi):
        return jnp.dot(hn, w_ref[:, lo:hi], preferred_element_type=F32)

    seg_a = project(0, OFF_BQ)
    seg_bq = project(OFF_BQ, OFF_BK)

    ca, sa = cos_a[...], sin_a[...]
    cq = _rms(seg_a[:, OFF_CQ:OFF_CQ + A_RANK], gq[...]).astype(BF16)
    q = jnp.dot(cq, wuq[...], preferred_element_type=F32)
    for h in range(A_HEADS):
        lo = h * A_DK
        qa[:, lo:lo + 128] = (q[:, lo:lo + 128] * SCALE_A).astype(BF16)
        qa[:, lo + 128:lo + 256] = (_rope(q[:, lo + 128:lo + 256], ca, sa) * SCALE_A).astype(BF16)
    ckv = _rms(seg_a[:, OFF_CKV:OFF_CKV + A_RANK], gkv[...]).astype(BF16)
    kv = jnp.dot(ckv, wukv[...], preferred_element_type=F32)
    kr = _rope(seg_a[:, OFF_KR:OFF_KR + 128], ca, sa).astype(BF16)
    for h in range(A_HEADS):
        lo = h * A_DK
        ka[:, lo:lo + 128] = kv[:, h * 128:(h + 1) * 128].astype(BF16)
        ka[:, lo + 128:lo + 256] = kr
    va[...] = kv[:, A_WIDTH:2 * A_WIDTH].astype(BF16)

    cb, sb = cos_b[...], sin_b[...]
    sel = sel_ref[...]
    ones = jnp.ones((bm, 128), BF16)

    def mixer_b(seg, scale, use_rope, is_v, o1, o4, o16):
        width = 256 if is_v else 128
        tiles = []
        for h in range(B_HEADS):
            x = seg[:, h * 128:(h + 1) * 128]
            if use_rope:
                x = _rope(x, cb, sb)
            if scale is not None:
                x = x * scale
            tiles.append(x.astype(BF16))
            scr[h] = x
            o1[:, h * width:h * width + 128] = tiles[h]
            if is_v:
                o1[:, h * width + 128:(h + 1) * width] = ones
        y = jnp.dot(sel, jnp.concatenate(tiles, axis=1), preferred_element_type=F32)
        for d, od in ((4, o4), (16, o16)):
            n = bm // d
            for h in range(B_HEADS):
                for r in range(d):
                    c0 = (h * d + r) * width
                    if d == 4:
                        rows = scr[h, pl.ds(r, n, stride=d), :]
                    else:
                        rows = y[r * n:(r + 1) * n, h * 128:(h + 1) * 128]
                    od[:, c0:c0 + 128] = rows.astype(BF16)
                    if is_v:
                        od[:, c0 + 128:c0 + 256] = ones[0:n]

    seg_bk = project(OFF_BK, OFF_BV)
    mixer_b(seg_bq, SCALE_B, True, False, qb1, qb4, qb16)
    seg_bv = project(OFF_BV, OFF_CQH)
    mixer_b(seg_bk, None, True, False, kb1, kb4, kb16)
    seg_c = project(OFF_CQH, PROJ_WIDTH)
    mixer_b(seg_bv, None, False, True, vb1, vb4, vb16)

    cc, sc = cos_c[...], sin_c[...]
    for h in range(C_HEADS):
        x = _rms(seg_c[:, h * 128:(h + 1) * 128], gcq[...])
        qc[:, h * 128:(h + 1) * 128] = (_rope(x, cc, sc) * SCALE_C).astype(BF16)
    for h in range(C_KV_HEADS):
        lo = C_WIDTH + h * 128
        x = _rms(seg_c[:, lo:lo + 128], gck[...])
        kc[:, h * 128:(h + 1) * 128] = _rope(x, cc, sc).astype(BF16)
    vc[...] = seg_c[:, C_WIDTH + C_KV_WIDTH:].astype(BF16)


def _proj_prep(x, g, w, tables, gq, gkv, gcq, gck, wuq, wukv):
    t = x.shape[0]
    bm = PREP_BM
    nblk_seq = SEQ // bm
    row = lambda i: (i, 0)
    tab = lambda i: (i % nblk_seq, 0)
    out_widths = [A_HEADS * A_DK, A_HEADS * A_DK, A_WIDTH]
    out_shapes = [jax.ShapeDtypeStruct((t, w_), BF16) for w_ in out_widths]
    out_specs = [pl.BlockSpec((bm, w_), row) for w_ in out_widths]
    for d in B_DILATIONS:
        for w_ in (B_WIDTH, B_WIDTH, 2 * B_WIDTH):
            out_shapes.append(jax.ShapeDtypeStruct((t // d, d * w_), BF16))
            out_specs.append(pl.BlockSpec((bm // d, d * w_), row))
    for w_ in (C_WIDTH, C_KV_WIDTH, C_KV_WIDTH):
        out_shapes.append(jax.ShapeDtypeStruct((t, w_), BF16))
        out_specs.append(pl.BlockSpec((bm, w_), row))
    in_specs = [pl.BlockSpec((bm, D_MODEL), row), _resident((1, D_MODEL)),
                _resident((D_MODEL, PROJ_WIDTH))]
    in_specs += [pl.BlockSpec((bm, 128), tab)] * 6
    in_specs += [_resident((1, A_RANK)), _resident((1, A_RANK)),
                 _resident((1, 128)), _resident((1, 128)),
                 _resident((A_RANK, A_HEADS * A_DK)), _resident((A_RANK, 2 * A_WIDTH)),
                 _resident((bm, bm))]
    return pl.pallas_call(
        _proj_prep_kernel,
        grid=(t // bm,),
        in_specs=in_specs,
        out_specs=out_specs,
        out_shape=out_shapes,
        scratch_shapes=[pltpu.VMEM((B_HEADS, bm, LANES), F32)],
        compiler_params=pltpu.CompilerParams(
            dimension_semantics=("parallel",), vmem_limit_bytes=VMEM_LIMIT["proj_prep"]),
        name="proj_prep",
    )(x, g, w, *tables, gq, gkv, gcq, gck, wuq, wukv, _stream_select(bm))


def _stream_select(bm):
    d = B_DILATIONS[2]
    out_row = jnp.arange(bm)
    src = (out_row % (bm // d)) * d + out_row // (bm // d)
    return jax.nn.one_hot(src, bm, dtype=BF16)


ATTN_KC = 1024
ATTN_SUB = 512
ATTN_BLOCKS = 4


def _attn_kernel(*refs, group, dk, dv, bq, n_side):
    n_blk, kc, sub = ATTN_BLOCKS, ATTN_KC, ATTN_SUB
    q_ref, k_ref, v_ref = refs[0:3]
    side_in = refs[3:3 + n_side]
    o_ref = refs[3 + n_side]
    side_out = refs[4 + n_side:4 + 2 * n_side]
    vt_scr = refs[4 + 2 * n_side]

    for src, dst in zip(side_in, side_out):
        dst[...] = src[...].astype(BF16)

    @pl.when(pl.program_id(2) == 0)
    def _():
        vt_scr[...] = v_ref[0].T

    q_blocks = [jnp.concatenate([q_ref[0, i * bq:(i + 1) * bq, g * dk:(g + 1) * dk]
                                 for g in range(group)], axis=0) for i in range(n_blk)]
    n_chunks = SEQ // kc
    n_sub = kc // sub

    def scores(item, j):
        blk, c = item
        r0 = c * kc + j * sub
        return lax.dot_general(k_ref[0, r0:r0 + sub, :], q_blocks[blk], (((1,), (1,)), ((), ())),
                               preferred_element_type=F32)

    def fold8(x, op):
        return op(x.reshape(x.shape[0] // 8, 8, x.shape[1]), axis=0)

    def col_max(pieces):
        mx = fold8(pieces[0], jnp.max)
        for piece in pieces[1:]:
            mx = jnp.maximum(mx, fold8(piece, jnp.max))
        return jnp.max(mx, axis=0, keepdims=True)

    items = [(blk, c) for blk in range(n_blk) for c in range(n_chunks)]
    s_cur = [scores(items[0], j) for j in range(n_sub)]
    m = l = acc = None
    for idx, (blk, c) in enumerate(items):
        m_c = col_max(s_cur)
        m_new = m_c if c == 0 else jnp.maximum(m, m_c)
        s_next, p_sum, p_bf = [], None, []
        for j in range(n_sub):
            if idx + 1 < len(items):
                s_next.append(scores(items[idx + 1], j))
            p = jnp.exp2(s_cur[j] - m_new)
            p_sum = fold8(p, jnp.sum) if j == 0 else p_sum + fold8(p, jnp.sum)
            p_bf.append(p.astype(BF16))
        l_c = jnp.sum(p_sum, axis=0, keepdims=True)
        pv = jnp.dot(vt_scr[:, c * kc:(c + 1) * kc], jnp.concatenate(p_bf, axis=0),
                     preferred_element_type=F32)
        if c == 0:
            l, acc = l_c, pv
        else:
            alpha = jnp.exp2(m - m_new)
            l = alpha * l + l_c
            acc = alpha * acc + pv
        m = m_new
        s_cur = s_next
        if c == n_chunks - 1:
            out = acc / l
            for g in range(group):
                o_ref[0, blk * bq:(blk + 1) * bq, g * dv:(g + 1) * dv] = (
                    out[:, g * bq:(g + 1) * bq].T)


def _attention(q, k, v, *, kv_heads, group, dk, dv, bq, name, side_casts=()):
    b = q.shape[0]
    rows_q = ATTN_BLOCKS * bq
    n_q = SEQ // rows_q

    def side_block(rows_total, rows):
        last = rows_total // rows - 1
        return lambda bi, h, qi: jnp.minimum((bi * kv_heads + h) * n_q + qi, last)

    in_specs = [
        pl.BlockSpec((1, rows_q, group * dk), lambda bi, h, qi: (bi, qi, h)),
        pl.BlockSpec((1, SEQ, dk), lambda bi, h, qi: (bi, 0, h)),
        pl.BlockSpec((1, SEQ, dv), lambda bi, h, qi: (bi, 0, h)),
    ]
    out_specs = [pl.BlockSpec((1, rows_q, group * dv), lambda bi, h, qi: (bi, qi, h))]
    out_shapes = [jax.ShapeDtypeStruct((b, SEQ, kv_heads * group * dv), F32)]
    operands = [q, k, v]
    n_steps = b * kv_heads * n_q
    for w, layer in side_casts:
        _, r, c = w.shape
        rows = next(m for m in range(BF16_SUBLANES, r + 1, BF16_SUBLANES)
                    if r % m == 0 and r // m <= n_steps)
        blk = side_block(r, rows)
        in_specs.append(pl.BlockSpec(
            (None, rows, c), lambda bi, h, qi, blk=blk, layer=layer: (layer, blk(bi, h, qi), 0)))
        out_specs.append(pl.BlockSpec((rows, c), lambda bi, h, qi, blk=blk: (blk(bi, h, qi), 0)))
        out_shapes.append(jax.ShapeDtypeStruct((r, c), BF16))
        operands.append(w)
    return pl.pallas_call(
        functools.partial(_attn_kernel, group=group, dk=dk, dv=dv, bq=bq, n_side=len(side_casts)),
        grid=(b, kv_heads, n_q),
        in_specs=in_specs,
        out_specs=out_specs,
        out_shape=out_shapes,
        scratch_shapes=[pltpu.VMEM((dv, SEQ), BF16)],
        compiler_params=pltpu.CompilerParams(
            dimension_semantics=("arbitrary", "arbitrary", "arbitrary"),
            vmem_limit_bytes=VMEM_LIMIT["attention"]),
        name=name,
    )(*operands)


DIL_BQ = 128
DIL_WIN = DIL_BQ + 2 * B_HALF


def _dil_block(q_ref, k_ref, v_ref, bias_ref, stream, q0, length):
    if isinstance(q0, int):
        start = min(max(q0 - B_HALF, 0), length - DIL_WIN)
        case = (q0 - start) // B_HALF
    else:
        start = pl.multiple_of(jnp.clip(q0 - B_HALF, 0, length - DIL_WIN), B_HALF)
        case = lax.shift_right_logical(q0 - start, 6)
    q = q_ref[0, pl.ds(q0, DIL_BQ), stream * 128:(stream + 1) * 128]
    k = k_ref[0, pl.ds(start, DIL_WIN), stream * 128:(stream + 1) * 128]
    vx = v_ref[0, pl.ds(start, DIL_WIN), stream * 256:(stream + 1) * 256]
    s = lax.dot_general(q, k, (((1,), (1,)), ((), ())), preferred_element_type=F32)
    s = s + bias_ref[case]
    m = jnp.max(s, axis=-1, keepdims=True)
    p = jnp.exp2(s - m).astype(BF16)
    ol = jnp.dot(p, vx, preferred_element_type=F32)
    l = ol[:, 128:256]
    return ol[:, 0:128] / l, m + jnp.log(l) * LOG2E


def _dil_kernel(q1, k1, v1, q4, k4, v4, q16, k16, v16, bias_ref, o_ref,
                osm4, lsm4, osm16, lsm16, otok, ltok):
    def body1(n, carry):
        q0 = pl.multiple_of(n * DIL_BQ, DIL_BQ)
        o, lse = _dil_block(q1, k1, v1, bias_ref, 0, q0, SEQ)
        otok[0, pl.ds(q0, DIL_BQ), :] = o
        ltok[0, pl.ds(q0, DIL_BQ), :] = lse
        return carry

    lax.fori_loop(0, SEQ // DIL_BQ, body1, 0, unroll=True)

    def body4(n, carry):
        q0 = pl.multiple_of(n * DIL_BQ, DIL_BQ)
        for r in range(4):
            o, lse = _dil_block(q4, k4, v4, bias_ref, r, q0, SEQ // 4)
            osm4[pl.ds(q0, DIL_BQ), r * 128:(r + 1) * 128] = o
            lsm4[pl.ds(q0, DIL_BQ), r * 128:(r + 1) * 128] = lse
        return carry

    lax.fori_loop(0, SEQ // 4 // DIL_BQ, body4, 0, unroll=True)
    for r in range(4):
        otok[1, pl.ds(r, SEQ // 4, stride=4), :] = osm4[:, r * 128:(r + 1) * 128]
        ltok[1, pl.ds(r, SEQ // 4, stride=4), :] = lsm4[:, r * 128:(r + 1) * 128]

    for n in range(SEQ // 16 // DIL_BQ):
        q0 = n * DIL_BQ
        for r in range(16):
            o, lse = _dil_block(q16, k16, v16, bias_ref, r, q0, SEQ // 16)
            osm16[q0:q0 + DIL_BQ, r * 128:(r + 1) * 128] = o
            lsm16[q0:q0 + DIL_BQ, r * 128:(r + 1) * 128] = lse
    for r in range(16):
        otok[2, pl.ds(r, SEQ // 16, stride=16), :] = osm16[:, r * 128:(r + 1) * 128]
        ltok[2, pl.ds(r, SEQ // 16, stride=16), :] = lsm16[:, r * 128:(r + 1) * 128]

    chunk = 256

    def combine(c, carry):
        r0 = pl.multiple_of(c * chunk, chunk)
        l0 = ltok[0, pl.ds(r0, chunk), :]
        l1 = ltok[1, pl.ds(r0, chunk), :]
        l2 = ltok[2, pl.ds(r0, chunk), :]
        mx = jnp.maximum(jnp.maximum(l0, l1), l2)
        e0 = jnp.exp2(l0 - mx)
        e1 = jnp.exp2(l1 - mx)
        e2 = jnp.exp2(l2 - mx)
        num = (e0 * otok[0, pl.ds(r0, chunk), :] + e1 * otok[1, pl.ds(r0, chunk), :]
               + e2 * otok[2, pl.ds(r0, chunk), :])
        o_ref[0, pl.ds(r0, chunk), :] = num / (e0 + e1 + e2)
        return carry

    lax.fori_loop(0, SEQ // chunk, combine, 0)


def _dil_bias():
    i = jnp.arange(DIL_BQ)[:, None]
    j = jnp.arange(DIL_WIN)[None, :]
    return jnp.stack([jnp.where(jnp.abs(j - i - c * B_HALF) <= B_HALF, 0.0, NEG)
                      for c in range(3)]).astype(F32)


def _dilated(qkv_by_dilation):
    b = qkv_by_dilation[0].shape[0]
    in_specs = []
    for d in B_DILATIONS:
        in_specs += [pl.BlockSpec((1, SEQ // d, d * 128), lambda bi, h: (bi, 0, h))] * 2
        in_specs += [pl.BlockSpec((1, SEQ // d, d * 256), lambda bi, h: (bi, 0, h))]
    in_specs += [pl.BlockSpec((3, DIL_BQ, DIL_WIN), lambda bi, h: (0, 0, 0))]
    qkv_by_dilation = list(qkv_by_dilation) + [_dil_bias()]
    return pl.pallas_call(
        _dil_kernel,
        grid=(b, B_HEADS),
        in_specs=in_specs,
        out_specs=pl.BlockSpec((1, SEQ, 128), lambda bi, h: (bi, 0, h)),
        out_shape=jax.ShapeDtypeStruct((b, SEQ, B_WIDTH), F32),
        scratch_shapes=[
            pltpu.VMEM((SEQ // 4, 4 * 128), F32), pltpu.VMEM((SEQ // 4, 4 * 128), F32),
            pltpu.VMEM((SEQ // 16, 16 * 128), F32), pltpu.VMEM((SEQ // 16, 16 * 128), F32),
            pltpu.VMEM((3, SEQ, 128), F32), pltpu.VMEM((3, SEQ, 128), F32),
        ],
        compiler_params=pltpu.CompilerParams(
            dimension_semantics=("parallel", "parallel"), vmem_limit_bytes=VMEM_LIMIT["dilated"]),
        name="dilated_attn",
    )(*qkv_by_dilation)


def _out_proj_kernel(ya, yb, yc, x_ref, g_ref, w_ref, o_ref):
    b0, c0 = A_WIDTH, A_WIDTH + B_WIDTH
    na = _rms(ya[...], g_ref[:, 0:b0]).astype(BF16)
    nb = _rms(yb[...], g_ref[:, b0:c0]).astype(BF16)
    nc = _rms(yc[...], g_ref[:, c0:]).astype(BF16)
    y = (jnp.dot(na, w_ref[0:b0, :], preferred_element_type=F32)
         + jnp.dot(nb, w_ref[b0:c0, :], preferred_element_type=F32)
         + jnp.dot(nc, w_ref[c0:, :], preferred_element_type=F32))
    o_ref[...] = x_ref[...] + y


def _out_proj(ya, yb, yc, x, g, w):
    t = x.shape[0]
    bm = OUT_BM
    row = lambda i: (i, 0)
    return pl.pallas_call(
        _out_proj_kernel,
        grid=(t // bm,),
        in_specs=[
            pl.BlockSpec((bm, A_WIDTH), row),
            pl.BlockSpec((bm, B_WIDTH), row),
            pl.BlockSpec((bm, C_WIDTH), row),
            pl.BlockSpec((bm, D_MODEL), row),
            _resident((1, D_MODEL)),
            _resident((D_MODEL, D_MODEL)),
        ],
        out_specs=pl.BlockSpec((bm, D_MODEL), row),
        out_shape=jax.ShapeDtypeStruct((t, D_MODEL), F32),
        compiler_params=pltpu.CompilerParams(
            dimension_semantics=("parallel",), vmem_limit_bytes=VMEM_LIMIT["out_proj"]),
        name="out_proj",
    )(ya, yb, yc, x, g, w)


def _ffn_kernel(x_ref, g_ref, wg_ref, wu_ref, wd_ref, fg_ref, o_ref, h_scr, *, final_norm):
    j = pl.program_id(1)

    @pl.when(j == 0)
    def _():
        x = x_ref[...]
        h_scr[...] = _rms(x, g_ref[...]).astype(BF16)
        o_ref[...] = x

    h = h_scr[...]
    gate = jnp.dot(h, wg_ref[...], preferred_element_type=F32)
    up = jnp.dot(h, wu_ref[...], preferred_element_type=F32)
    ff = (gate * jax.nn.sigmoid(gate)) * up
    o_ref[...] += jnp.dot(ff.astype(BF16), wd_ref[...], preferred_element_type=F32)

    if final_norm:
        @pl.when(j == pl.num_programs(1) - 1)
        def _():
            o_ref[...] = _rms(o_ref[...], fg_ref[...])


def _ffn(x, g, wg, wu, wd, fg, *, final_norm):
    t = x.shape[0]
    bm, bf = FFN_BM, FFN_BF
    return pl.pallas_call(
        functools.partial(_ffn_kernel, final_norm=final_norm),
        grid=(t // bm, D_FF // bf),
        in_specs=[
            pl.BlockSpec((bm, D_MODEL), lambda i, j: (i, 0)),
            pl.BlockSpec((1, D_MODEL), lambda i, j: (0, 0)),
            pl.BlockSpec((D_MODEL, bf), lambda i, j: (0, j)),
            pl.BlockSpec((D_MODEL, bf), lambda i, j: (0, j)),
            pl.BlockSpec((bf, D_MODEL), lambda i, j: (j, 0)),
            pl.BlockSpec((1, D_MODEL), lambda i, j: (0, 0)),
        ],
        out_specs=pl.BlockSpec((bm, D_MODEL), lambda i, j: (i, 0)),
        out_shape=jax.ShapeDtypeStruct((t, D_MODEL), F32),
        scratch_shapes=[pltpu.VMEM((bm, D_MODEL), BF16)],
        compiler_params=pltpu.CompilerParams(
            dimension_semantics=("parallel", "arbitrary"), vmem_limit_bytes=VMEM_LIMIT["ffn"]),
        name="swiglu_ffn",
    )(x, g, wg, wu, wd, fg)


def _rope_tables():
    pos = jnp.arange(SEQ, dtype=jnp.int32)

    def angles(p, dim):
        inv = ROPE_THETA ** (-jnp.arange(0, dim, 2, dtype=F32) / dim)
        return p.astype(F32)[:, None] * inv[None, :]

    ang_b = angles(pos, HEAD_DIM)
    ang_a = angles(pos, A_ROPE)
    ang_g = angles(jnp.arange(GRID_W, dtype=jnp.int32), HEAD_DIM // 2)
    cos_g, sin_g = jnp.cos(ang_g), jnp.sin(ang_g)
    cos_r, sin_r = (jnp.repeat(a, GRID_W, axis=0) for a in (cos_g, sin_g))
    cos_k, sin_k = (jnp.tile(a, (SEQ // GRID_W, 1)) for a in (cos_g, sin_g))
    cos_pa, sin_pa = jnp.cos(ang_a), jnp.sin(ang_a)
    cos_pb, sin_pb = jnp.cos(ang_b), jnp.sin(ang_b)
    z = jnp.zeros((SEQ, 32), F32)
    cat = lambda *xs: jnp.concatenate(xs, axis=-1)
    cos_a = cat(cos_pa, z, cos_pa, z)
    sin_a = cat(-sin_pa, z, sin_pa, z)
    cos_b = cat(cos_pb, cos_pb)
    sin_b = cat(-sin_pb, sin_pb)
    cos_c = cat(cos_r, cos_k, cos_r, cos_k)
    sin_c = cat(-sin_r, -sin_k, sin_r, sin_k)
    return cos_a, sin_a, cos_b, sin_b, cos_c, sin_c


def _axial_perm():
    a = jnp.arange(32)
    return jnp.concatenate([a, a + 64, a + 32, a + 96])


def _w_in_layout_kernel(w_ref, o_ref):
    bl = w_ref.shape[1]

    def put(col, rows):
        o_ref[:, col:col + 128] = rows.T.astype(BF16)

    for t in range(OFF_KR // 128):
        put(t * 128, w_ref[t * 128:(t + 1) * 128, :])
    z = jnp.zeros((32, bl), F32)
    put(OFF_KR, jnp.concatenate([w_ref[1024:1056, :], z, w_ref[1056:1088, :], z], axis=0))
    for t in range(3 * B_WIDTH // 128):
        put(OFF_BQ + t * 128, w_ref[1088 + t * 128:1088 + (t + 1) * 128, :])
    for h in range(C_HEADS + C_KV_HEADS):
        r0 = 3392 + h * 128
        put(OFF_CQH + h * 128, jnp.concatenate(
            [w_ref[r0:r0 + 32, :], w_ref[r0 + 64:r0 + 96, :],
             w_ref[r0 + 32:r0 + 64, :], w_ref[r0 + 96:r0 + 128, :]], axis=0))
    for t in range(C_KV_WIDTH // 128):
        put(OFF_CV + t * 128, w_ref[4416 + t * 128:4416 + (t + 1) * 128, :])


def _layout_w_in(w, layer):
    wt = jnp.swapaxes(w, 1, 2)
    bl = 256
    width = wt.shape[1]
    return pl.pallas_call(
        _w_in_layout_kernel,
        grid=(D_MODEL // bl,),
        in_specs=[pl.BlockSpec((None, width, bl), lambda i: (layer, 0, i))],
        out_specs=pl.BlockSpec((bl, PROJ_WIDTH), lambda i: (i, 0)),
        out_shape=jax.ShapeDtypeStruct((D_MODEL, PROJ_WIDTH), BF16),
        compiler_params=pltpu.CompilerParams(
            dimension_semantics=("parallel",), vmem_limit_bytes=VMEM_LIMIT["w_in_layout"]),
        name="w_in_layout",
    )(wt)


def _layout_w_uq(w):
    w = w.reshape(A_RANK, A_HEADS, A_NOPE + A_ROPE)
    z = jnp.zeros((A_RANK, A_HEADS, 32), w.dtype)
    out = jnp.concatenate([w[:, :, :128], w[:, :, 128:160], z, w[:, :, 160:192], z], axis=2)
    return out.reshape(A_RANK, A_HEADS * A_DK).astype(BF16)


def _layout_w_ukv(w):
    w = w.reshape(A_RANK, A_HEADS, 2 * HEAD_DIM)
    out = jnp.concatenate([w[:, :, :128].reshape(A_RANK, A_WIDTH),
                           w[:, :, 128:].reshape(A_RANK, A_WIDTH)], axis=1)
    return out.astype(BF16)


def kernel(x, attn_norm, w_in, a_q_norm, a_w_uq, a_kv_norm, a_w_ukv, c_q_norm, c_k_norm,
           out_norm, w_out, ffn_norm, w_gate, w_up, w_down, final_norm):
    bsz, seq, _ = x.shape
    t = bsz * seq
    depth = w_in.shape[0]
    tables = _rope_tables()
    perm = _axial_perm()
    xf = x.reshape(t, D_MODEL)
    for l in range(depth):
        (qa, ka, va, qb1, kb1, vb1, qb4, kb4, vb4, qb16, kb16, vb16, qc, kc, vc) = _proj_prep(
            xf, attn_norm[l][None, :], _layout_w_in(w_in, l), tables,
            a_q_norm[l][None, :], a_kv_norm[l][None, :],
            c_q_norm[l][perm][None, :], c_k_norm[l][perm][None, :],
            _layout_w_uq(a_w_uq[l]), _layout_w_ukv(a_w_ukv[l]))
        r3 = lambda a, n=1: a.reshape(bsz, seq // n, a.shape[1])
        ya, wd_bf, wo_bf = _attention(
            r3(qa), r3(ka), r3(va), kv_heads=A_HEADS, group=1, dk=A_DK, dv=HEAD_DIM, bq=MLA_BQ,
            name="mla_attn", side_casts=((w_down, l), (w_out, l)))
        yb = _dilated([r3(qb1), r3(kb1), r3(vb1), r3(qb4, 4), r3(kb4, 4), r3(vb4, 4),
                       r3(qb16, 16), r3(kb16, 16), r3(vb16, 16)])
        yc, wg_bf, wu_bf = _attention(
            r3(qc), r3(kc), r3(vc), kv_heads=C_KV_HEADS, group=C_GROUP, dk=HEAD_DIM, dv=HEAD_DIM,
            bq=GQA_BQ, name="gqa_attn", side_casts=((w_gate, l), (w_up, l)))
        xf = _out_proj(ya.reshape(t, A_WIDTH), yb.reshape(t, B_WIDTH), yc.reshape(t, C_WIDTH),
                       xf, out_norm[l][None, :], wo_bf)
        xf = _ffn(xf, ffn_norm[l][None, :], wg_bf, wu_bf, wd_bf, final_norm[None, :],
                  final_norm=(l == depth - 1))
    return xf.reshape(bsz, seq, D_MODEL)
```

```python
import functools

import jax
import jax.numpy as jnp
from jax import lax
from jax.experimental import pallas as pl
from jax.experimental.pallas import tpu as pltpu

F32 = jnp.float32
BF16 = jnp.bfloat16
MIX_DTYPE = BF16

D_MODEL = 2048
SEQ = 4096
HEAD_DIM = 128
ROPE_THETA = 10000.0
GRID_W = 64
EPS = 1e-6
NEG = -1e30

A_HEADS = 4
A_RANK = 512
A_NOPE = 128
A_ROPE = 64
A_DK = 256
B_HEADS = 6
B_DILATIONS = (1, 4, 16)
B_HALF = 64
C_HEADS = 6
C_KV_HEADS = 2
C_GROUP = C_HEADS // C_KV_HEADS
A_WIDTH = A_HEADS * HEAD_DIM
B_WIDTH = B_HEADS * HEAD_DIM
C_WIDTH = C_HEADS * HEAD_DIM
C_KV_WIDTH = C_KV_HEADS * HEAD_DIM
D_FF = 5632

OFF_CQ = 0
OFF_CKV = 512
OFF_KR = 1024
OFF_BQ = 1152
OFF_BK = OFF_BQ + B_WIDTH
OFF_BV = OFF_BK + B_WIDTH
OFF_CQH = OFF_BV + B_WIDTH
OFF_CK = OFF_CQH + C_WIDTH
OFF_CV = OFF_CK + C_KV_WIDTH
PROJ_WIDTH = OFF_CV + C_KV_WIDTH

LOG2E = 1.4426950408889634
SCALE_A = (A_NOPE + A_ROPE) ** -0.5 * LOG2E
SCALE_B = HEAD_DIM ** -0.5 * LOG2E
SCALE_C = HEAD_DIM ** -0.5 * LOG2E

LANES = 128
BF16_SUBLANES = 16
MIB = 1024 * 1024
V7X_VMEM_BYTES = 64 * MIB

PREP_BM = 256
OUT_BM = 512
MLA_BQ = 512
GQA_BQ = 256
FFN_BM = 1024
FFN_BF = 512
VMEM_LIMIT = {
    "proj_prep": 56 * MIB,
    "attention": 48 * MIB,
    "dilated": 56 * MIB,
    "out_proj": 48 * MIB,
    "ffn": 60 * MIB,
    "w_in_layout": 40 * MIB,
}
assert max(VMEM_LIMIT.values()) < V7X_VMEM_BYTES


def _rms(x, g):
    ms = jnp.mean(x * x, axis=-1, keepdims=True)
    return x * lax.rsqrt(ms + EPS) * g


def _rope(x, c, s):
    return x * c + pltpu.roll(x, LANES // 2, 1) * s


def _resident(shape):
    return pl.BlockSpec(shape, lambda i: (0,) * len(shape), pipeline_mode=pl.Buffered(1))


def _proj_prep_kernel(x_ref, g_ref, w_ref, cos_a, sin_a, cos_b, sin_b, cos_c, sin_c,
                      gq, gkv, gcq, gck, wuq, wukv, sel_ref,
                      qa, ka, va, qb1, kb1, vb1, qb4, kb4, vb4, qb16, kb16, vb16, qc, kc, vc):
    bm = PREP_BM
    hn = _rms(x_ref[...], g_ref[...]).astype(BF16)

    def project(lo, hi):
        return jnp.dot(hn, w_ref[:, lo:hi], preferred_element_type=F32)

    seg_a = project(0, OFF_BQ)
    seg_bq = project(OFF_BQ, OFF_BK)

    ca, sa = cos_a[...], sin_a[...]
    cq = _rms(seg_a[:, OFF_CQ:OFF_CQ + A_RANK], gq[...]).astype(BF16)
    q = jnp.dot(cq, wuq[...], preferred_element_type=F32)
    for h in range(A_HEADS):
        lo = h * A_DK
        qa[:, lo:lo + 128] = (q[:, lo:lo + 128] * SCALE_A).astype(BF16)
        qa[:, lo + 128:lo + 256] = (_rope(q[:, lo + 128:lo + 256], ca, sa) * SCALE_A).astype(BF16)
    ckv = _rms(seg_a[:, OFF_CKV:OFF_CKV + A_RANK], gkv[...]).astype(BF16)
    kv = jnp.dot(ckv, wukv[...], preferred_element_type=F32)
    kr = _rope(seg_a[:, OFF_KR:OFF_KR + 128], ca, sa).astype(BF16)
    for h in range(A_HEADS):
        lo = h * A_DK
        ka[:, lo:lo + 128] = kv[:, h * 128:(h + 1) * 128].astype(BF16)
        ka[:, lo + 128:lo + 256] = kr
    va[...] = kv[:, A_WIDTH:2 * A_WIDTH].astype(BF16)

    cb, sb = cos_b[...], sin_b[...]
    sel = sel_ref[...]
    ones = jnp.ones((bm, 128), BF16)

    def mixer_b(seg, scale, use_rope, is_v, o1, o4, o16):
        width = 256 if is_v else 128
        tiles = []
        for h in range(B_HEADS):
            x = seg[:, h * 128:(h + 1) * 128]
            if use_rope:
                x = _rope(x, cb, sb)
            if scale is not None:
                x = x * scale
            tiles.append(x.astype(BF16))
            o1[:, h * width:h * width + 128] = tiles[h]
            if is_v:
                o1[:, h * width + 128:(h + 1) * width] = ones
        y = jnp.dot(sel, jnp.concatenate(tiles, axis=1), preferred_element_type=F32)
        for d, od, base in ((4, o4, 0), (16, o16, bm)):
            n = bm // d
            for h in range(B_HEADS):
                for r in range(d):
                    c0 = (h * d + r) * width
                    rows = y[base + r * n:base + (r + 1) * n, h * 128:(h + 1) * 128]
                    od[:, c0:c0 + 128] = rows.astype(BF16)
                    if is_v:
                        od[:, c0 + 128:c0 + 256] = ones[0:n]

    seg_bk = project(OFF_BK, OFF_BV)
    mixer_b(seg_bq, SCALE_B, True, False, qb1, qb4, qb16)
    seg_bv = project(OFF_BV, OFF_CQH)
    mixer_b(seg_bk, None, True, False, kb1, kb4, kb16)
    seg_c = project(OFF_CQH, PROJ_WIDTH)
    mixer_b(seg_bv, None, False, True, vb1, vb4, vb16)

    cc, sc = cos_c[...], sin_c[...]
    for h in range(C_HEADS):
        x = _rms(seg_c[:, h * 128:(h + 1) * 128], gcq[...])
        qc[:, h * 128:(h + 1) * 128] = (_rope(x, cc, sc) * SCALE_C).astype(BF16)
    for h in range(C_KV_HEADS):
        lo = C_WIDTH + h * 128
        x = _rms(seg_c[:, lo:lo + 128], gck[...])
        kc[:, h * 128:(h + 1) * 128] = _rope(x, cc, sc).astype(BF16)
    vc[...] = seg_c[:, C_WIDTH + C_KV_WIDTH:].astype(BF16)


def _proj_prep(x, g, w, tables, gq, gkv, gcq, gck, wuq, wukv):
    t = x.shape[0]
    bm = PREP_BM
    nblk_seq = SEQ // bm
    row = lambda i: (i, 0)
    tab = lambda i: (i % nblk_seq, 0)
    out_widths = [A_HEADS * A_DK, A_HEADS * A_DK, A_WIDTH]
    out_shapes = [jax.ShapeDtypeStruct((t, w_), BF16) for w_ in out_widths]
    out_specs = [pl.BlockSpec((bm, w_), row) for w_ in out_widths]
    for d in B_DILATIONS:
        for w_ in (B_WIDTH, B_WIDTH, 2 * B_WIDTH):
            out_shapes.append(jax.ShapeDtypeStruct((t // d, d * w_), BF16))
            out_specs.append(pl.BlockSpec((bm // d, d * w_), row))
    for w_ in (C_WIDTH, C_KV_WIDTH, C_KV_WIDTH):
        out_shapes.append(jax.ShapeDtypeStruct((t, w_), BF16))
        out_specs.append(pl.BlockSpec((bm, w_), row))
    in_specs = [pl.BlockSpec((bm, D_MODEL), row), _resident((1, D_MODEL)),
                _resident((D_MODEL, PROJ_WIDTH))]
    in_specs += [pl.BlockSpec((bm, 128), tab)] * 6
    in_specs += [_resident((1, A_RANK)), _resident((1, A_RANK)),
                 _resident((1, 128)), _resident((1, 128)),
                 _resident((A_RANK, A_HEADS * A_DK)), _resident((A_RANK, 2 * A_WIDTH)),
                 _resident((2 * bm, bm))]
    return pl.pallas_call(
        _proj_prep_kernel,
        grid=(t // bm,),
        in_specs=in_specs,
        out_specs=out_specs,
        out_shape=out_shapes,
        compiler_params=pltpu.CompilerParams(
            dimension_semantics=("parallel",), vmem_limit_bytes=VMEM_LIMIT["proj_prep"]),
        name="proj_prep",
    )(x, g, w, *tables, gq, gkv, gcq, gck, wuq, wukv, _stream_select(bm))


def _stream_select(bm):
    blocks = []
    for d in B_DILATIONS[1:]:
        out_row = jnp.arange(bm)
        src = (out_row % (bm // d)) * d + out_row // (bm // d)
        blocks.append(jax.nn.one_hot(src, bm, dtype=BF16))
    return jnp.concatenate(blocks, axis=0)


ATTN_KC = 1024
ATTN_SUB = 512
ATTN_BLOCKS = 4


def _attn_kernel(*refs, group, dk, dv, bq, n_side):
    n_blk, kc, sub = ATTN_BLOCKS, ATTN_KC, ATTN_SUB
    q_ref, k_ref, v_ref = refs[0:3]
    side_in = refs[3:3 + n_side]
    o_ref = refs[3 + n_side]
    side_out = refs[4 + n_side:4 + 2 * n_side]
    vt_scr = refs[4 + 2 * n_side]

    for src, dst in zip(side_in, side_out):
        dst[...] = src[...].astype(BF16)

    @pl.when(pl.program_id(2) == 0)
    def _():
        vt_scr[...] = v_ref[0].T

    q_blocks = [jnp.concatenate([q_ref[0, i * bq:(i + 1) * bq, g * dk:(g + 1) * dk]
                                 for g in range(group)], axis=0) for i in range(n_blk)]
    n_chunks = SEQ // kc
    n_sub = kc // sub

    def scores(item, j):
        blk, c = item
        r0 = c * kc + j * sub
        return lax.dot_general(k_ref[0, r0:r0 + sub, :], q_blocks[blk], (((1,), (1,)), ((), ())),
                               preferred_element_type=F32)

    def fold8(x, op):
        return op(x.reshape(x.shape[0] // 8, 8, x.shape[1]), axis=0)

    def col_max(pieces):
        mx = fold8(pieces[0], jnp.max)
        for piece in pieces[1:]:
            mx = jnp.maximum(mx, fold8(piece, jnp.max))
        return jnp.max(mx, axis=0, keepdims=True)

    items = [(blk, c) for blk in range(n_blk) for c in range(n_chunks)]
    s_cur = [scores(items[0], j) for j in range(n_sub)]
    m = l = acc = None
    for idx, (blk, c) in enumerate(items):
        m_c = col_max(s_cur)
        m_new = m_c if c == 0 else jnp.maximum(m, m_c)
        s_next, p_sum, p_bf = [], None, []
        for j in range(n_sub):
            if idx + 1 < len(items):
                s_next.append(scores(items[idx + 1], j))
            p = jnp.exp2(s_cur[j] - m_new)
            p_sum = fold8(p, jnp.sum) if j == 0 else p_sum + fold8(p, jnp.sum)
            p_bf.append(p.astype(BF16))
        l_c = jnp.sum(p_sum, axis=0, keepdims=True)
        pv = jnp.dot(vt_scr[:, c * kc:(c + 1) * kc], jnp.concatenate(p_bf, axis=0),
                     preferred_element_type=F32)
        if c == 0:
            l, acc = l_c, pv
        else:
            alpha = jnp.exp2(m - m_new)
            l = alpha * l + l_c
            acc = alpha * acc + pv
        m = m_new
        s_cur = s_next
        if c == n_chunks - 1:
            out = acc / l
            for g in range(group):
                o_ref[0, blk * bq:(blk + 1) * bq, g * dv:(g + 1) * dv] = (
                    out[:, g * bq:(g + 1) * bq].T.astype(o_ref.dtype))


def _attention(q, k, v, *, kv_heads, group, dk, dv, bq, name, side_casts=()):
    b = q.shape[0]
    rows_q = ATTN_BLOCKS * bq
    n_q = SEQ // rows_q

    def side_block(rows_total, rows):
        last = rows_total // rows - 1
        return lambda bi, h, qi: jnp.minimum((bi * kv_heads + h) * n_q + qi, last)

    in_specs = [
        pl.BlockSpec((1, rows_q, group * dk), lambda bi, h, qi: (bi, qi, h)),
        pl.BlockSpec((1, SEQ, dk), lambda bi, h, qi: (bi, 0, h)),
        pl.BlockSpec((1, SEQ, dv), lambda bi, h, qi: (bi, 0, h)),
    ]
    out_specs = [pl.BlockSpec((1, rows_q, group * dv), lambda bi, h, qi: (bi, qi, h))]
    out_shapes = [jax.ShapeDtypeStruct((b, SEQ, kv_heads * group * dv), MIX_DTYPE)]
    operands = [q, k, v]
    n_steps = b * kv_heads * n_q
    for w, layer in side_casts:
        _, r, c = w.shape
        rows = next(m for m in range(BF16_SUBLANES, r + 1, BF16_SUBLANES)
                    if r % m == 0 and r // m <= n_steps)
        blk = side_block(r, rows)
        in_specs.append(pl.BlockSpec(
            (None, rows, c), lambda bi, h, qi, blk=blk, layer=layer: (layer, blk(bi, h, qi), 0)))
        out_specs.append(pl.BlockSpec((rows, c), lambda bi, h, qi, blk=blk: (blk(bi, h, qi), 0)))
        out_shapes.append(jax.ShapeDtypeStruct((r, c), BF16))
        operands.append(w)
    return pl.pallas_call(
        functools.partial(_attn_kernel, group=group, dk=dk, dv=dv, bq=bq, n_side=len(side_casts)),
        grid=(b, kv_heads, n_q),
        in_specs=in_specs,
        out_specs=out_specs,
        out_shape=out_shapes,
        scratch_shapes=[pltpu.VMEM((dv, SEQ), BF16)],
        compiler_params=pltpu.CompilerParams(
            dimension_semantics=("arbitrary", "arbitrary", "arbitrary"),
            vmem_limit_bytes=VMEM_LIMIT["attention"]),
        name=name,
    )(*operands)


DIL_BQ = 128
DIL_WIN = DIL_BQ + 2 * B_HALF


def _dil_block(q_ref, k_ref, v_ref, bias_ref, stream, q0, length):
    if isinstance(q0, int):
        start = min(max(q0 - B_HALF, 0), length - DIL_WIN)
        case = (q0 - start) // B_HALF
    else:
        start = pl.multiple_of(jnp.clip(q0 - B_HALF, 0, length - DIL_WIN), B_HALF)
        case = lax.shift_right_logical(q0 - start, 6)
    q = q_ref[0, pl.ds(q0, DIL_BQ), stream * 128:(stream + 1) * 128]
    k = k_ref[0, pl.ds(start, DIL_WIN), stream * 128:(stream + 1) * 128]
    vx = v_ref[0, pl.ds(start, DIL_WIN), stream * 256:(stream + 1) * 256]
    s = lax.dot_general(q, k, (((1,), (1,)), ((), ())), preferred_element_type=F32)
    s = s + bias_ref[case]
    m = jnp.max(s, axis=-1, keepdims=True)
    p = jnp.exp2(s - m).astype(BF16)
    ol = jnp.dot(p, vx, preferred_element_type=F32)
    l = ol[:, 128:256]
    return ol[:, 0:128] / l, m + jnp.log(l) * LOG2E


def _dil_kernel(q1, k1, v1, q4, k4, v4, q16, k16, v16, bias_ref, o_ref,
                osm4, lsm4, osm16, lsm16, otok, ltok):
    def body1(n, carry):
        q0 = pl.multiple_of(n * DIL_BQ, DIL_BQ)
        o, lse = _dil_block(q1, k1, v1, bias_ref, 0, q0, SEQ)
        otok[0, pl.ds(q0, DIL_BQ), :] = o
        ltok[0, pl.ds(q0, DIL_BQ), :] = lse
        return carry

    lax.fori_loop(0, SEQ // DIL_BQ, body1, 0, unroll=True)

    def body4(n, carry):
        q0 = pl.multiple_of(n * DIL_BQ, DIL_BQ)
        for r in range(4):
            o, lse = _dil_block(q4, k4, v4, bias_ref, r, q0, SEQ // 4)
            osm4[pl.ds(q0, DIL_BQ), r * 128:(r + 1) * 128] = o
            lsm4[pl.ds(q0, DIL_BQ), r * 128:(r + 1) * 128] = lse
        return carry

    lax.fori_loop(0, SEQ // 4 // DIL_BQ, body4, 0, unroll=True)
    for r in range(4):
        otok[1, pl.ds(r, SEQ // 4, stride=4), :] = osm4[:, r * 128:(r + 1) * 128]
        ltok[1, pl.ds(r, SEQ // 4, stride=4), :] = lsm4[:, r * 128:(r + 1) * 128]

    for n in range(SEQ // 16 // DIL_BQ):
        q0 = n * DIL_BQ
        for r in range(16):
            o, lse = _dil_block(q16, k16, v16, bias_ref, r, q0, SEQ // 16)
            osm16[q0:q0 + DIL_BQ, r * 128:(r + 1) * 128] = o
            lsm16[q0:q0 + DIL_BQ, r * 128:(r + 1) * 128] = lse
    for r in range(16):
        otok[2, pl.ds(r, SEQ // 16, stride=16), :] = osm16[:, r * 128:(r + 1) * 128]
        ltok[2, pl.ds(r, SEQ // 16, stride=16), :] = lsm16[:, r * 128:(r + 1) * 128]

    chunk = 256

    def combine(c, carry):
        r0 = pl.multiple_of(c * chunk, chunk)
        l0 = ltok[0, pl.ds(r0, chunk), :]
        l1 = ltok[1, pl.ds(r0, chunk), :]
        l2 = ltok[2, pl.ds(r0, chunk), :]
        mx = jnp.maximum(jnp.maximum(l0, l1), l2)
        e0 = jnp.exp2(l0 - mx)
        e1 = jnp.exp2(l1 - mx)
        e2 = jnp.exp2(l2 - mx)
        num = (e0 * otok[0, pl.ds(r0, chunk), :] + e1 * otok[1, pl.ds(r0, chunk), :]
               + e2 * otok[2, pl.ds(r0, chunk), :])
        o_ref[0, pl.ds(r0, chunk), :] = (num / (e0 + e1 + e2)).astype(o_ref.dtype)
        return carry

    lax.fori_loop(0, SEQ // chunk, combine, 0)


def _dil_bias():
    i = jnp.arange(DIL_BQ)[:, None]
    j = jnp.arange(DIL_WIN)[None, :]
    return jnp.stack([jnp.where(jnp.abs(j - i - c * B_HALF) <= B_HALF, 0.0, NEG)
                      for c in range(3)]).astype(F32)


def _dilated(qkv_by_dilation):
    b = qkv_by_dilation[0].shape[0]
    in_specs = []
    for d in B_DILATIONS:
        in_specs += [pl.BlockSpec((1, SEQ // d, d * 128), lambda bi, h: (bi, 0, h))] * 2
        in_specs += [pl.BlockSpec((1, SEQ // d, d * 256), lambda bi, h: (bi, 0, h))]
    in_specs += [pl.BlockSpec((3, DIL_BQ, DIL_WIN), lambda bi, h: (0, 0, 0))]
    qkv_by_dilation = list(qkv_by_dilation) + [_dil_bias()]
    return pl.pallas_call(
        _dil_kernel,
        grid=(b, B_HEADS),
        in_specs=in_specs,
        out_specs=pl.BlockSpec((1, SEQ, 128), lambda bi, h: (bi, 0, h)),
        out_shape=jax.ShapeDtypeStruct((b, SEQ, B_WIDTH), MIX_DTYPE),
        scratch_shapes=[
            pltpu.VMEM((SEQ // 4, 4 * 128), F32), pltpu.VMEM((SEQ // 4, 4 * 128), F32),
            pltpu.VMEM((SEQ // 16, 16 * 128), F32), pltpu.VMEM((SEQ // 16, 16 * 128), F32),
            pltpu.VMEM((3, SEQ, 128), F32), pltpu.VMEM((3, SEQ, 128), F32),
        ],
        compiler_params=pltpu.CompilerParams(
            dimension_semantics=("parallel", "parallel"), vmem_limit_bytes=VMEM_LIMIT["dilated"]),
        name="dilated_attn",
    )(*qkv_by_dilation)


def _out_proj_kernel(ya, yb, yc, x_ref, g_ref, w_ref, o_ref):
    b0, c0 = A_WIDTH, A_WIDTH + B_WIDTH
    na = _rms(ya[...].astype(F32), g_ref[:, 0:b0]).astype(BF16)
    nb = _rms(yb[...].astype(F32), g_ref[:, b0:c0]).astype(BF16)
    nc = _rms(yc[...].astype(F32), g_ref[:, c0:]).astype(BF16)
    y = (jnp.dot(na, w_ref[0:b0, :], preferred_element_type=F32)
         + jnp.dot(nb, w_ref[b0:c0, :], preferred_element_type=F32)
         + jnp.dot(nc, w_ref[c0:, :], preferred_element_type=F32))
    o_ref[...] = x_ref[...] + y


def _out_proj(ya, yb, yc, x, g, w):
    t = x.shape[0]
    bm = OUT_BM
    row = lambda i: (i, 0)
    return pl.pallas_call(
        _out_proj_kernel,
        grid=(t // bm,),
        in_specs=[
            pl.BlockSpec((bm, A_WIDTH), row),
            pl.BlockSpec((bm, B_WIDTH), row),
            pl.BlockSpec((bm, C_WIDTH), row),
            pl.BlockSpec((bm, D_MODEL), row),
            _resident((1, D_MODEL)),
            _resident((D_MODEL, D_MODEL)),
        ],
        out_specs=pl.BlockSpec((bm, D_MODEL), row),
        out_shape=jax.ShapeDtypeStruct((t, D_MODEL), F32),
        compiler_params=pltpu.CompilerParams(
            dimension_semantics=("parallel",), vmem_limit_bytes=VMEM_LIMIT["out_proj"]),
        name="out_proj",
    )(ya, yb, yc, x, g, w)


def _ffn_kernel(x_ref, g_ref, wg_ref, wu_ref, wd_ref, fg_ref, o_ref, h_scr, *, final_norm):
    j = pl.program_id(1)

    @pl.when(j == 0)
    def _():
        x = x_ref[...]
        h_scr[...] = _rms(x, g_ref[...]).astype(BF16)
        o_ref[...] = x

    h = h_scr[...]
    gate = jnp.dot(h, wg_ref[...], preferred_element_type=F32)
    up = jnp.dot(h, wu_ref[...], preferred_element_type=F32)
    ff = (gate * jax.nn.sigmoid(gate)) * up
    o_ref[...] += jnp.dot(ff.astype(BF16), wd_ref[...], preferred_element_type=F32)

    if final_norm:
        @pl.when(j == pl.num_programs(1) - 1)
        def _():
            o_ref[...] = _rms(o_ref[...], fg_ref[...])


def _ffn(x, g, wg, wu, wd, fg, *, final_norm):
    t = x.shape[0]
    bm, bf = FFN_BM, FFN_BF
    return pl.pallas_call(
        functools.partial(_ffn_kernel, final_norm=final_norm),
        grid=(t // bm, D_FF // bf),
        in_specs=[
            pl.BlockSpec((bm, D_MODEL), lambda i, j: (i, 0)),
            pl.BlockSpec((1, D_MODEL), lambda i, j: (0, 0)),
            pl.BlockSpec((D_MODEL, bf), lambda i, j: (0, j)),
            pl.BlockSpec((D_MODEL, bf), lambda i, j: (0, j)),
            pl.BlockSpec((bf, D_MODEL), lambda i, j: (j, 0)),
            pl.BlockSpec((1, D_MODEL), lambda i, j: (0, 0)),
        ],
        out_specs=pl.BlockSpec((bm, D_MODEL), lambda i, j: (i, 0)),
        out_shape=jax.ShapeDtypeStruct((t, D_MODEL), F32),
        scratch_shapes=[pltpu.VMEM((bm, D_MODEL), BF16)],
        compiler_params=pltpu.CompilerParams(
            dimension_semantics=("parallel", "arbitrary"), vmem_limit_bytes=VMEM_LIMIT["ffn"]),
        name="swiglu_ffn",
    )(x, g, wg, wu, wd, fg)


def _rope_tables():
    pos = jnp.arange(SEQ, dtype=jnp.int32)

    def angles(p, dim):
        inv = ROPE_THETA ** (-jnp.arange(0, dim, 2, dtype=F32) / dim)
        return p.astype(F32)[:, None] * inv[None, :]

    ang_b = angles(pos, HEAD_DIM)
    ang_a = angles(pos, A_ROPE)
    ang_g = angles(jnp.arange(GRID_W, dtype=jnp.int32), HEAD_DIM // 2)
    cos_g, sin_g = jnp.cos(ang_g), jnp.sin(ang_g)
    cos_r, sin_r = (jnp.repeat(a, GRID_W, axis=0) for a in (cos_g, sin_g))
    cos_k, sin_k = (jnp.tile(a, (SEQ // GRID_W, 1)) for a in (cos_g, sin_g))
    cos_pa, sin_pa = jnp.cos(ang_a), jnp.sin(ang_a)
    cos_pb, sin_pb = jnp.cos(ang_b), jnp.sin(ang_b)
    z = jnp.zeros((SEQ, 32), F32)
    cat = lambda *xs: jnp.concatenate(xs, axis=-1)
    cos_a = cat(cos_pa, z, cos_pa, z)
    sin_a = cat(-sin_pa, z, sin_pa, z)
    cos_b = cat(cos_pb, cos_pb)
    sin_b = cat(-sin_pb, sin_pb)
    cos_c = cat(cos_r, cos_k, cos_r, cos_k)
    sin_c = cat(-sin_r, -sin_k, sin_r, sin_k)
    return cos_a, sin_a, cos_b, sin_b, cos_c, sin_c


def _axial_perm():
    a = jnp.arange(32)
    return jnp.concatenate([a, a + 64, a + 32, a + 96])


def _w_in_layout_kernel(w_ref, o_ref):
    bl = w_ref.shape[1]

    def put(col, rows):
        o_ref[:, col:col + 128] = rows.T.astype(BF16)

    for t in range(OFF_KR // 128):
        put(t * 128, w_ref[t * 128:(t + 1) * 128, :])
    z = jnp.zeros((32, bl), F32)
    put(OFF_KR, jnp.concatenate([w_ref[1024:1056, :], z, w_ref[1056:1088, :], z], axis=0))
    for t in range(3 * B_WIDTH // 128):
        put(OFF_BQ + t * 128, w_ref[1088 + t * 128:1088 + (t + 1) * 128, :])
    for h in range(C_HEADS + C_KV_HEADS):
        r0 = 3392 + h * 128
        put(OFF_CQH + h * 128, jnp.concatenate(
            [w_ref[r0:r0 + 32, :], w_ref[r0 + 64:r0 + 96, :],
             w_ref[r0 + 32:r0 + 64, :], w_ref[r0 + 96:r0 + 128, :]], axis=0))
    for t in range(C_KV_WIDTH // 128):
        put(OFF_CV + t * 128, w_ref[4416 + t * 128:4416 + (t + 1) * 128, :])


def _layout_w_in(w, layer):
    wt = jnp.swapaxes(w, 1, 2)
    bl = 256
    width = wt.shape[1]
    return pl.pallas_call(
        _w_in_layout_kernel,
        grid=(D_MODEL // bl,),
        in_specs=[pl.BlockSpec((None, width, bl), lambda i: (layer, 0, i))],
        out_specs=pl.BlockSpec((bl, PROJ_WIDTH), lambda i: (i, 0)),
        out_shape=jax.ShapeDtypeStruct((D_MODEL, PROJ_WIDTH), BF16),
        compiler_params=pltpu.CompilerParams(
            dimension_semantics=("parallel",), vmem_limit_bytes=VMEM_LIMIT["w_in_layout"]),
        name="w_in_layout",
    )(wt)


def _layout_w_uq(w):
    w = w.reshape(A_RANK, A_HEADS, A_NOPE + A_ROPE)
    z = jnp.zeros((A_RANK, A_HEADS, 32), w.dtype)
    out = jnp.concatenate([w[:, :, :128], w[:, :, 128:160], z, w[:, :, 160:192], z], axis=2)
    return out.reshape(A_RANK, A_HEADS * A_DK).astype(BF16)


def _layout_w_ukv(w):
    w = w.reshape(A_RANK, A_HEADS, 2 * HEAD_DIM)
    out = jnp.concatenate([w[:, :, :128].reshape(A_RANK, A_WIDTH),
                           w[:, :, 128:].reshape(A_RANK, A_WIDTH)], axis=1)
    return out.astype(BF16)


def kernel(x, attn_norm, w_in, a_q_norm, a_w_uq, a_kv_norm, a_w_ukv, c_q_norm, c_k_norm,
           out_norm, w_out, ffn_norm, w_gate, w_up, w_down, final_norm):
    bsz, seq, _ = x.shape
    t = bsz * seq
    depth = w_in.shape[0]
    tables = _rope_tables()
    perm = _axial_perm()
    xf = x.reshape(t, D_MODEL)
    for l in range(depth):
        (qa, ka, va, qb1, kb1, vb1, qb4, kb4, vb4, qb16, kb16, vb16, qc, kc, vc) = _proj_prep(
            xf, attn_norm[l][None, :], _layout_w_in(w_in, l), tables,
            a_q_norm[l][None, :], a_kv_norm[l][None, :],
            c_q_norm[l][perm][None, :], c_k_norm[l][perm][None, :],
            _layout_w_uq(a_w_uq[l]), _layout_w_ukv(a_w_ukv[l]))
        r3 = lambda a, n=1: a.reshape(bsz, seq // n, a.shape[1])
        ya, wd_bf, wo_bf = _attention(
            r3(qa), r3(ka), r3(va), kv_heads=A_HEADS, group=1, dk=A_DK, dv=HEAD_DIM, bq=MLA_BQ,
            name="mla_attn", side_casts=((w_down, l), (w_out, l)))
        yb = _dilated([r3(qb1), r3(kb1), r3(vb1), r3(qb4, 4), r3(kb4, 4), r3(vb4, 4),
                       r3(qb16, 16), r3(kb16, 16), r3(vb16, 16)])
        yc, wg_bf, wu_bf = _attention(
            r3(qc), r3(kc), r3(vc), kv_heads=C_KV_HEADS, group=C_GROUP, dk=HEAD_DIM, dv=HEAD_DIM,
            bq=GQA_BQ, name="gqa_attn", side_casts=((w_gate, l), (w_up, l)))
        xf = _out_proj(ya.reshape(t, A_WIDTH), yb.reshape(t, B_WIDTH), yc.reshape(t, C_WIDTH),
                       xf, out_norm[l][None, :], wo_bf)
        xf = _ffn(xf, ffn_norm[l][None, :], wg_bf, wu_bf, wd_bf, final_norm[None, :],
                  final_norm=(l == depth - 1))
    return xf.reshape(bsz, seq, D_MODEL)
```
